```python
import math
import jax, jax.numpy as jnp
from jax import lax
import numpy as np

D_MODEL = 1024
BATCH = 2
SEQ = 16384
DEPTH = 1
DEC_BATCH = 32
DEC_SEQ = 64
PAST_LEN = 1024

CHUNK = 64
N_META = 16
N_HEADS = 8
Q_LORA = 384
KV_LORA = 256
QK_NOPE = 128
QK_ROPE = 64
V_HEAD = 128
ATTN_WIDTH = N_HEADS * V_HEAD
POOL_WINDOWS = (2, 4, 8, 16)
N_POOL_GROUPS = 4
POOL_GROUP = 128
D_POOL = N_POOL_GROUPS * POOL_GROUP
POOL_HIST = max(POOL_WINDOWS) - 1
D_FF = 4 * D_MODEL
D_IN = Q_LORA + KV_LORA + QK_ROPE + D_POOL + 2 * D_MODEL
ROPE_BASE = 10000.0
EPS = 1e-6
Q_BLOCK = 128
SM_SCALE = (QK_NOPE + QK_ROPE) ** -0.5
NEG_INF = -1e30

kernel_name = 'mla_pool_gated_hybrid_stream_step'


def rmsnorm(x, g):
    xf = x.astype(jnp.float32)
    y = xf * lax.rsqrt(jnp.mean(xf * xf, axis=-1, keepdims=True) + EPS)
    return (y * g.astype(jnp.float32)).astype(x.dtype)


def rope(x, pos):
    half = QK_ROPE // 2
    inv_freq = jnp.exp(-math.log(ROPE_BASE) * jnp.arange(half, dtype=jnp.float32) / half)
    ang = pos.astype(jnp.float32)[:, None] * inv_freq[None, :]
    cos = jnp.cos(ang)[:, None, :]
    sin = jnp.sin(ang)[:, None, :]
    xf = x.astype(jnp.float32)
    x1, x2 = xf[..., :half], xf[..., half:]
    return jnp.concatenate([x1 * cos - x2 * sin, x1 * sin + x2 * cos], axis=-1).astype(x.dtype)


def mixer_inputs(xn, pos, w_in, g_kv):
    z = xn @ w_in
    idx = np.cumsum([Q_LORA, KV_LORA, QK_ROPE, D_POOL]).tolist()
    q_lat, kv_lat, k_raw, u_pool, gate_logits = jnp.split(z, idx, axis=-1)
    c_kv = rmsnorm(kv_lat, g_kv)
    k_pe = rope(k_raw[:, :, None, :], pos)[:, :, 0, :]
    return q_lat, c_kv, k_pe, u_pool, gate_logits


def queries(q_lat, pos, g_q, w_q_up, w_uk):
    B, T, _ = q_lat.shape
    q = (rmsnorm(q_lat, g_q) @ w_q_up).reshape(B, T, N_HEADS, QK_NOPE + QK_ROPE)
    q_abs = jnp.einsum('bthn,chn->bthc', q[..., :QK_NOPE], w_uk)
    q_pe = rope(q[..., QK_NOPE:], pos)
    return q_abs, q_pe


def attend(q_abs, q_pe, c_kv, k_pe, w_uv, mask=None):
    s = (jnp.einsum('bqhc,bkc->bhqk', q_abs, c_kv)
         + jnp.einsum('bqhr,bkr->bhqk', q_pe, k_pe)).astype(jnp.float32) * SM_SCALE
    if mask is not None:
        s = jnp.where(mask, s, NEG_INF)
    p = jax.nn.softmax(s, axis=-1).astype(c_kv.dtype)
    o_lat = jnp.einsum('bhqk,bkc->bqhc', p, c_kv)
    return jnp.einsum('bqhc,chv->bqhv', o_lat, w_uv)


def attention_prompt(q_abs, q_pe, c_kv, k_pe, w_uv, with_meta):
    B, L, _ = c_kv.shape
    n_frames = L - N_META
    key_chunk = jnp.concatenate([jnp.full((N_META,), -1, jnp.int32),
                                 jnp.arange(n_frames, dtype=jnp.int32) // CHUNK])
    q_off = N_META if with_meta else 0

    def block(i):
        start = q_off + i * Q_BLOCK
        qa = lax.dynamic_slice_in_dim(q_abs, start, Q_BLOCK, axis=1)
        qp = lax.dynamic_slice_in_dim(q_pe, start, Q_BLOCK, axis=1)
        q_chunk = (i * Q_BLOCK + jnp.arange(Q_BLOCK, dtype=jnp.int32)) // CHUNK
        mask = key_chunk[None, :] <= q_chunk[:, None]
        return attend(qa, qp, c_kv, k_pe, w_uv, mask)

    o = lax.map(block, jnp.arange(n_frames // Q_BLOCK, dtype=jnp.int32))
    o = jnp.moveaxis(o, 0, 1).reshape(B, n_frames, N_HEADS, V_HEAD)
    if with_meta:
        o_meta = attend(q_abs[:, :N_META], q_pe[:, :N_META], c_kv[:, :N_META], k_pe[:, :N_META], w_uv)
        o = jnp.concatenate([o_meta, o], axis=1)
    return o


def multiscale_pool(u_ext, n_hist, n_out, w_pool_grp, pool_scale):
    B, total, _ = u_ext.shape
    uf = u_ext.astype(jnp.float32)
    cs = jnp.concatenate([jnp.zeros((B, 1, D_POOL), jnp.float32), jnp.cumsum(uf, axis=1)], axis=1)
    j = jnp.arange(total - n_out, total, dtype=jnp.int32) - POOL_HIST
    outs = []
    for g, w in enumerate(POOL_WINDOWS):
        sl = slice(g * POOL_GROUP, (g + 1) * POOL_GROUP)
        win_sum = cs[:, total - n_out + 1:total + 1, sl] - cs[:, total - n_out + 1 - w:total + 1 - w, sl]
        cnt = jnp.minimum(w, n_hist + j + 1).astype(jnp.float32)
        outs.append(win_sum / cnt[None, :, None] - uf[:, total - n_out:, sl])
    pooled = jnp.stack(outs, axis=2).astype(u_ext.dtype)
    mixed = jnp.einsum('btgc,gcd->btgd', pooled, w_pool_grp)
    return mixed.reshape(B, n_out, D_POOL) * pool_scale


def merge_and_ffn(h, o_attn, pooled, gate_logits, w_attn_br, w_pool_br, w_out, g_norm_ffn, w_up, w_down):
    B, T, _ = h.shape
    a = o_attn.reshape(B, T, ATTN_WIDTH) @ w_attn_br
    p = pooled @ w_pool_br
    gates = jax.nn.sigmoid(gate_logits.astype(jnp.float32)).astype(h.dtype)
    g_a, g_p = gates[..., :D_MODEL], gates[..., D_MODEL:]
    h = h + (g_a * a + g_p * p) @ w_out
    hn = rmsnorm(h, g_norm_ffn)
    return h + jnp.square(jax.nn.relu(hn @ w_up)) @ w_down


def setup_inputs(seed: int = 0) -> dict:
    key = jax.random.key(seed)
    ks = jax.random.split(key, 24)
    nrm = lambda k, shape, s=1.0: jax.random.normal(k, shape, jnp.float32) * s
    gain = lambda k, shape: 1.0 + 0.1 * jax.random.normal(k, shape, jnp.float32)
    return {
        'x_prompt': nrm(ks[0], (BATCH, SEQ, D_MODEL)),
        'x_sample': nrm(ks[1], (DEC_BATCH, DEC_SEQ, D_MODEL)),
        'cache_kv_latent': nrm(ks[2], (DEPTH, DEC_BATCH, PAST_LEN, KV_LORA)),
        'cache_k_rope': nrm(ks[3], (DEPTH, DEC_BATCH, PAST_LEN, QK_ROPE)),
        'cache_pool': nrm(ks[4], (DEPTH, DEC_BATCH, POOL_HIST, D_POOL)),
        'meta_tokens': nrm(ks[5], (N_META, D_MODEL)),
        'w_in': nrm(ks[6], (DEPTH, D_MODEL, D_IN), D_MODEL ** -0.5),
        'g_norm_mix': gain(ks[7], (DEPTH, D_MODEL)),
        'g_q': gain(ks[8], (DEPTH, Q_LORA)),
        'g_kv': gain(ks[9], (DEPTH, KV_LORA)),
        'w_q_up': nrm(ks[10], (DEPTH, Q_LORA, N_HEADS * (QK_NOPE + QK_ROPE)), Q_LORA ** -0.5),
        'w_uk': nrm(ks[11], (DEPTH, KV_LORA, N_HEADS, QK_NOPE), KV_LORA ** -0.5),
        'w_uv': nrm(ks[12], (DEPTH, KV_LORA, N_HEADS, V_HEAD), KV_LORA ** -0.5),
        'w_attn_br': nrm(ks[13], (DEPTH, ATTN_WIDTH, D_MODEL), ATTN_WIDTH ** -0.5),
        'w_pool_grp': nrm(ks[14], (DEPTH, N_POOL_GROUPS, POOL_GROUP, POOL_GROUP), POOL_GROUP ** -0.5),
        'pool_scale': gain(ks[15], (DEPTH, D_POOL)),
        'w_pool_br': nrm(ks[16], (DEPTH, D_POOL, D_MODEL), D_POOL ** -0.5),
        'w_out': nrm(ks[17], (DEPTH, D_MODEL, D_MODEL), D_MODEL ** -0.5),
        'g_norm_ffn': gain(ks[18], (DEPTH, D_MODEL)),
        'w_up': nrm(ks[19], (DEPTH, D_MODEL, D_FF), D_MODEL ** -0.5),
        'w_down': nrm(ks[20], (DEPTH, D_FF, D_MODEL), D_FF ** -0.5),
        'g_final': gain(ks[21], (D_MODEL,)),
    }


def reference(x_prompt, x_sample, cache_kv_latent, cache_k_rope, cache_pool, meta_tokens,
              w_in, g_norm_mix, g_q, g_kv, w_q_up, w_uk, w_uv, w_attn_br, w_pool_grp,
              pool_scale, w_pool_br, w_out, g_norm_ffn, w_up, w_down, g_final):
    B, S, _ = x_prompt.shape
    Bd, T, _ = x_sample.shape
    past = cache_kv_latent.shape[2]
    hp = jnp.concatenate([jnp.broadcast_to(meta_tokens[None], (B, N_META, D_MODEL)).astype(x_prompt.dtype),
                          x_prompt], axis=1)
    pos_p = jnp.arange(N_META + S, dtype=jnp.int32)
    hs = x_sample
    pos_s = past + jnp.arange(T, dtype=jnp.int32)
    c_p, pe_p, pool_p, c_s, pe_s, pool_s = [], [], [], [], [], []
    for l in range(DEPTH):
        last = l == DEPTH - 1
        r = N_META if last else 0
        xn = rmsnorm(hp, g_norm_mix[l])
        q_lat, c_kv, k_pe, u_pool, gate_logits = mixer_inputs(xn, pos_p, w_in[l], g_kv[l])
        q_abs, q_pe = queries(q_lat[:, r:], pos_p[r:], g_q[l], w_q_up[l], w_uk[l])
        o = attention_prompt(q_abs, q_pe, c_kv, k_pe, w_uv[l], not last)
        u_ext = jnp.concatenate([jnp.zeros((B, POOL_HIST, D_POOL), u_pool.dtype), u_pool], axis=1)
        pooled = multiscale_pool(u_ext, 0, hp.shape[1] - r, w_pool_grp[l], pool_scale[l])
        hp = merge_and_ffn(hp[:, r:], o, pooled, gate_logits[:, r:], w_attn_br[l], w_pool_br[l],
                           w_out[l], g_norm_ffn[l], w_up[l], w_down[l])
        c_p.append(c_kv)
        pe_p.append(k_pe)
        pool_p.append(u_pool[:, -POOL_HIST:])
        xn = rmsnorm(hs, g_norm_mix[l])
        q_lat, c_new, pe_new, u_new, gl = mixer_inputs(xn, pos_s, w_in[l], g_kv[l])
        q_abs, q_pe = queries(q_lat, pos_s, g_q[l], w_q_up[l], w_uk[l])
        c_all = jnp.concatenate([cache_kv_latent[l], c_new], axis=1)
        pe_all = jnp.concatenate([cache_k_rope[l], pe_new], axis=1)
        o = attend(q_abs, q_pe, c_all, pe_all, w_uv[l])
        u_ext = jnp.concatenate([cache_pool[l], u_new], axis=1)
        pooled = multiscale_pool(u_ext, POOL_HIST, T, w_pool_grp[l], pool_scale[l])
        hs = merge_and_ffn(hs, o, pooled, gl, w_attn_br[l], w_pool_br[l],
                           w_out[l], g_norm_ffn[l], w_up[l], w_down[l])
        c_s.append(c_new)
        pe_s.append(pe_new)
        pool_s.append(u_ext[:, -POOL_HIST:])
    y_prompt = rmsnorm(hp, g_final)
    y_sample = rmsnorm(hs, g_final)
    return (y_prompt, y_sample, jnp.stack(c_p), jnp.stack(pe_p), jnp.stack(pool_p),
            jnp.stack(c_s), jnp.stack(pe_s), jnp.stack(pool_s))
```

```python
import functools
import math

import jax
import jax.numpy as jnp
from jax import lax
from jax.experimental import pallas as pl
from jax.experimental.pallas import tpu as pltpu

F32 = jnp.float32
BF16 = jnp.bfloat16

D_MODEL = 1024
N_HEADS = 8
Q_LORA = 384
KV_LORA = 256
QK_NOPE = 128
QK_ROPE = 64
V_HEAD = 128
ATTN_WIDTH = N_HEADS * V_HEAD
POOL_WINDOWS = (2, 4, 8, 16)
POOL_GROUP = 128
D_POOL = len(POOL_WINDOWS) * POOL_GROUP
POOL_HIST = max(POOL_WINDOWS) - 1
HIST_ROWS = POOL_HIST + 1
D_FF = 4 * D_MODEL
N_META = 16
CHUNK = 64
ROPE_BASE = 10000.0
EPS = 1e-6
SM_SCALE = (QK_NOPE + QK_ROPE) ** -0.5
LOG2E = 1.4426950408889634
NEG_INF = -1e30

LANES = 128
HEAD_W = QK_NOPE + LANES
ROW_TILE = 512
META_PAD = 128
FF_CHUNK = 1024
VMEM_LIMIT = 56 * 1024 * 1024


def _nt(a, b):
    return lax.dot_general(a, b, (((1,), (1,)), ((), ())), preferred_element_type=F32)


def _mm(a, b):
    return jnp.dot(a, b, preferred_element_type=F32)


def _rms(x, g):
    return x * lax.rsqrt(jnp.mean(x * x, axis=-1, keepdims=True) + EPS) * g


def _const_spec(shape):
    nd = len(shape)
    return pl.BlockSpec(shape, lambda *_: (0,) * nd, pipeline_mode=pl.Buffered(1))


def _project_body(x_ref, cos_ref, sin_ref, gmix_ref, gq_ref, gkv_ref, wql_ref, wkvr_ref, wu_ref,
                  wg_ref, wqup_ref, wuk_ref, wuvt_ref, *outs, emit_q, emit_kv, emit_gates, emit_kpe2):
    outs = list(outs)
    q_ref = outs.pop(0) if emit_q else None
    k_ref, vt_ref = (outs.pop(0), outs.pop(0)) if emit_kv else (None, None)
    c_ref, kpe_ref, u_ref = outs.pop(0), outs.pop(0), outs.pop(0)
    g_ref = outs.pop(0) if emit_gates else None
    kpe2_ref = outs.pop(0) if emit_kpe2 else None

    rows = x_ref.shape[0]
    xn = _rms(x_ref[...], gmix_ref[...]).astype(BF16)
    cos = cos_ref[...]
    sin = sin_ref[...]
    lane = lax.broadcasted_iota(jnp.int32, (rows, LANES), 1)
    half_masks = (lane < QK_ROPE, lane >= QK_ROPE)

    kvr = _mm(xn, wkvr_ref[...])
    c = _rms(kvr[:, :KV_LORA], gkv_ref[...])
    c_ref[...] = c
    kpe2 = kvr[:, KV_LORA:KV_LORA + LANES] * cos + kvr[:, KV_LORA + LANES:] * sin
    kpe_ref[...] = kpe2[:, :QK_ROPE]
    if emit_kpe2:
        kpe2_ref[...] = kpe2.astype(BF16)
    cb = c.astype(BF16)

    if emit_kv:
        k_nope = _mm(cb, wuk_ref[...])
        for h in range(N_HEADS):
            k_ref[:, h * HEAD_W:h * HEAD_W + QK_NOPE] = k_nope[:, h * QK_NOPE:(h + 1) * QK_NOPE].astype(BF16)
            k_ref[:, h * HEAD_W + QK_NOPE:(h + 1) * HEAD_W] = jnp.where(half_masks[h % 2], kpe2, 0.0).astype(BF16)
        vt_ref[...] = _nt(wuvt_ref[...], cb).astype(BF16)

    u_ref[...] = _mm(xn, wu_ref[...])

    if emit_gates:
        logits = _mm(xn, wg_ref[...])
        g_ref[...] = (1.0 / (1.0 + jnp.exp(-logits))).astype(BF16)

    if emit_q:
        qn = _rms(_mm(xn, wql_ref[...]), gq_ref[...]).astype(BF16)
        qall = _mm(qn, wqup_ref[...]) * (SM_SCALE * LOG2E)
        nope_w = N_HEADS * QK_NOPE
        rope_w = N_HEADS * QK_ROPE
        for h in range(N_HEADS):
            j = h // 2
            r = qall[:, nope_w + j * LANES:nope_w + (j + 1) * LANES] * cos
            r = r + qall[:, nope_w + rope_w + j * LANES:nope_w + rope_w + (j + 1) * LANES] * sin
            q_ref[:, h * HEAD_W:h * HEAD_W + QK_NOPE] = qall[:, h * QK_NOPE:(h + 1) * QK_NOPE].astype(BF16)
            q_ref[:, h * HEAD_W + QK_NOPE:(h + 1) * HEAD_W] = jnp.where(half_masks[h % 2], r, 0.0).astype(BF16)


def _project(x, cos, sin, w, *, tile, emit_q, emit_kv, emit_gates, emit_kpe2):
    G, R, _ = x.shape
    nt = R // tile
    assert nt * tile == R
    row = lambda width: pl.BlockSpec((None, tile, width), lambda g, t: (g, t, 0))
    tab = pl.BlockSpec((tile, LANES), lambda g, t: (t, 0))
    weights = (w["g_mix"], w["g_q"], w["g_kv"], w["w_qlat"], w["w_kvr"], w["w_u"], w["w_g"],
               w["w_qup"], w["w_uk"], w["w_uvt"])
    in_specs = [row(D_MODEL), tab, tab] + [_const_spec(a.shape) for a in weights]
    out_shape, out_specs = [], []

    def add(shape, spec, dtype):
        out_shape.append(jax.ShapeDtypeStruct(shape, dtype))
        out_specs.append(spec)

    if emit_q:
        add((G, R, N_HEADS * HEAD_W), row(N_HEADS * HEAD_W), BF16)
    if emit_kv:
        add((G, nt, tile, N_HEADS * HEAD_W),
            pl.BlockSpec((None, None, tile, N_HEADS * HEAD_W), lambda g, t: (g, t, 0, 0)), BF16)
        add((G, nt, ATTN_WIDTH, tile),
            pl.BlockSpec((None, None, ATTN_WIDTH, tile), lambda g, t: (g, t, 0, 0)), BF16)
    add((G, R, KV_LORA), row(KV_LORA), F32)
    add((G, R, QK_ROPE), row(QK_ROPE), F32)
    add((G, R, D_POOL), row(D_POOL), F32)
    if emit_gates:
        add((G, R, 2 * D_MODEL), row(2 * D_MODEL), BF16)
    if emit_kpe2:
        add((G, R, LANES), row(LANES), BF16)

    body = functools.partial(_project_body, emit_q=emit_q, emit_kv=emit_kv, emit_gates=emit_gates,
                             emit_kpe2=emit_kpe2)
    outs = pl.pallas_call(
        body,
        grid=(G, nt),
        in_specs=in_specs,
        out_specs=out_specs,
        out_shape=out_shape,
        compiler_params=pltpu.CompilerParams(dimension_semantics=("parallel", "parallel"),
                                             vmem_limit_bytes=VMEM_LIMIT),
        name="project",
    )(x, cos, sin, *weights)
    outs = list(outs)
    res = {}
    if emit_q:
        res["q"] = outs.pop(0)
    if emit_kv:
        res["k"], res["vt"] = outs.pop(0), outs.pop(0)
    res["c"], res["kpe"], res["u"] = outs.pop(0), outs.pop(0), outs.pop(0)
    if emit_gates:
        res["gates"] = outs.pop(0)
    if emit_kpe2:
        res["kpe2"] = outs.pop(0)
    return res


def _flash_body(q_ref, k_ref, vt_ref, km_ref, vtm_ref, o_ref, m_ref, l_ref, acc_ref):
    qi = pl.program_id(2)
    q = q_ref[...]
    tk, tq = k_ref.shape[1], q.shape[0]

    def update(s_t, vt_blk, first=False):
        m_blk = jnp.max(s_t, axis=0, keepdims=True)
        if first:
            m_new = m_blk
        else:
            m_old = m_ref[...]
            m_new = jnp.maximum(m_old, m_blk)
        p = jnp.exp2(s_t - m_new)
        l_blk = jnp.sum(p, axis=0, keepdims=True)
        pv = _mm(vt_blk, p.astype(BF16))
        if first:
            l_ref[...] = l_blk
            acc_ref[...] = pv
        else:
            alpha = jnp.exp2(m_old - m_new)
            l_ref[...] = alpha * l_ref[...] + l_blk
            acc_ref[...] = alpha * acc_ref[...] + pv
        m_ref[...] = m_new

    s_t = _nt(k_ref[qi], q)
    key_chunk = lax.broadcasted_iota(jnp.int32, (tk, tq), 0) // CHUNK
    qry_chunk = lax.broadcasted_iota(jnp.int32, (tk, tq), 1) // CHUNK
    update(jnp.where(key_chunk <= qry_chunk, s_t, NEG_INF), vt_ref[qi], first=True)

    s_m = _nt(km_ref[...], q)
    real = lax.broadcasted_iota(jnp.int32, s_m.shape, 0) < N_META
    update(jnp.where(real, s_m, NEG_INF), vtm_ref[...])

    def body(kb, carry):
        update(_nt(k_ref[kb], q), vt_ref[kb])
        return carry

    lax.fori_loop(0, qi, body, 0)

    o_ref[...] = (acc_ref[...] / l_ref[...]).T.astype(BF16)


def _flash(q, k, vt, k_meta, vt_meta):
    B, S, _ = q.shape
    nt, tile = k.shape[1], k.shape[2]
    return pl.pallas_call(
        _flash_body,
        grid=(B, N_HEADS, nt),
        in_specs=[
            pl.BlockSpec((None, tile, HEAD_W), lambda b, h, i: (b, i, h)),
            pl.BlockSpec((None, nt, tile, HEAD_W), lambda b, h, i: (b, 0, 0, h)),
            pl.BlockSpec((None, nt, V_HEAD, tile), lambda b, h, i: (b, 0, h, 0)),
            pl.BlockSpec((None, None, META_PAD, HEAD_W), lambda b, h, i: (0, 0, 0, h)),
            pl.BlockSpec((None, None, V_HEAD, META_PAD), lambda b, h, i: (0, 0, h, 0)),
        ],
        out_specs=pl.BlockSpec((None, tile, V_HEAD), lambda b, h, i: (b, i, h)),
        out_shape=jax.ShapeDtypeStruct((B, S, ATTN_WIDTH), BF16),
        scratch_shapes=[pltpu.VMEM((1, tile), F32), pltpu.VMEM((1, tile), F32),
                        pltpu.VMEM((V_HEAD, tile), F32)],
        compiler_params=pltpu.CompilerParams(dimension_semantics=("parallel", "parallel", "arbitrary"),
                                             vmem_limit_bytes=VMEM_LIMIT),
        name="flash",
    )(q, k, vt, k_meta, vt_meta)


def _decode_body(q_ref, cn_ref, pn_ref, cc_ref, pc_ref, wukt_ref, wuv_ref, o_ref):
    q = q_ref[...]
    t = q.shape[0]
    qa = jnp.concatenate(
        [_mm(q[:, h * HEAD_W:h * HEAD_W + QK_NOPE], wukt_ref[h]) for h in range(N_HEADS)], axis=0).astype(BF16)
    qp = jnp.concatenate([q[:, h * HEAD_W + QK_NOPE:(h + 1) * HEAD_W] for h in range(N_HEADS)], axis=0)
    cc = cc_ref[...].astype(BF16)
    cn = cn_ref[...].astype(BF16)
    s_c = _nt(qa, cc) + _nt(qp, pc_ref[...])
    s_n = _nt(qa, cn) + _nt(qp, pn_ref[...])
    m = jnp.maximum(jnp.max(s_c, axis=-1, keepdims=True), jnp.max(s_n, axis=-1, keepdims=True))
    p_c = jnp.exp2(s_c - m)
    p_n = jnp.exp2(s_n - m)
    l = jnp.sum(p_c, axis=-1, keepdims=True) + jnp.sum(p_n, axis=-1, keepdims=True)
    o_lat = ((_mm(p_c.astype(BF16), cc) + _mm(p_n.astype(BF16), cn)) / l).astype(BF16)
    for h in range(N_HEADS):
        o_ref[:, h * V_HEAD:(h + 1) * V_HEAD] = _mm(o_lat[h * t:(h + 1) * t], wuv_ref[h]).astype(BF16)


def _decode(q, c_new, kpe2_new, cache_c, cache_pe2, wukt, wuv, *, seq):
    R = q.shape[0]
    nb, past, _ = cache_c.shape
    assert nb * seq == R
    return pl.pallas_call(
        _decode_body,
        grid=(nb,),
        in_specs=[
            pl.BlockSpec((seq, N_HEADS * HEAD_W), lambda i: (i, 0)),
            pl.BlockSpec((seq, KV_LORA), lambda i: (i, 0)),
            pl.BlockSpec((seq, LANES), lambda i: (i, 0)),
            pl.BlockSpec((None, past, KV_LORA), lambda i: (i, 0, 0)),
            pl.BlockSpec((None, past, LANES), lambda i: (i, 0, 0)),
            _const_spec(wukt.shape),
            _const_spec(wuv.shape),
        ],
        out_specs=pl.BlockSpec((seq, ATTN_WIDTH), lambda i: (i, 0)),
        out_shape=jax.ShapeDtypeStruct((R, ATTN_WIDTH), BF16),
        compiler_params=pltpu.CompilerParams(dimension_semantics=("parallel",),
                                             vmem_limit_bytes=VMEM_LIMIT),
        name="decode",
    )(q, c_new, kpe2_new, cache_c, cache_pe2, wukt, wuv)


def _pool_seq(hist, u):
    ext = jnp.concatenate([hist, u], axis=0)
    outs = []
    for g, w in enumerate(POOL_WINDOWS):
        e = ext[:, g * POOL_GROUP:(g + 1) * POOL_GROUP]
        s = e
        shift = 1
        while shift < w:
            s = s + pltpu.roll(s, shift, axis=0)
            shift *= 2
        outs.append(s[HIST_ROWS:] * (1.0 / w) - e[HIST_ROWS:])
    return outs


def _merge_body(*refs, n_seq, prompt):
    if prompt:
        (x_ref, o_ref, u_ref, hprev_ref, hmeta_ref, g_ref, wab_ref, wpg_ref, ps_ref, wpb_ref, wout_ref,
         gffn_ref, wup_ref, wdown_ref, gfin_ref, y_ref) = refs
        first = pl.program_id(1) == 0
        hists = [jnp.where(first, hmeta_ref[...], hprev_ref[...])]
    else:
        (x_ref, o_ref, u_ref, hist_ref, g_ref, wab_ref, wpg_ref, ps_ref, wpb_ref, wout_ref,
         gffn_ref, wup_ref, wdown_ref, gfin_ref, y_ref) = refs
        hists = [hist_ref[i] for i in range(n_seq)]

    rows = x_ref.shape[0]
    t = rows // n_seq
    u = u_ref[...]
    per_seq = [_pool_seq(hists[i], u[i * t:(i + 1) * t]) for i in range(n_seq)]
    mixed = []
    for g in range(len(POOL_WINDOWS)):
        pooled = jnp.concatenate([per_seq[i][g] for i in range(n_seq)], axis=0) if n_seq > 1 else per_seq[0][g]
        mixed.append(_mm(pooled.astype(BF16), wpg_ref[g]))
    pm = (jnp.concatenate(mixed, axis=-1) * ps_ref[...]).astype(BF16)
    p = _mm(pm, wpb_ref[...])
    a = _mm(o_ref[...], wab_ref[...])
    gates = g_ref[...].astype(F32)
    mix = (gates[:, :D_MODEL] * a + gates[:, D_MODEL:] * p).astype(BF16)
    h = x_ref[...] + _mm(mix, wout_ref[...])
    hn = _rms(h, gffn_ref[...]).astype(BF16)
    for c in range(D_FF // FF_CHUNK):
        up = jnp.maximum(_mm(hn, wup_ref[:, c * FF_CHUNK:(c + 1) * FF_CHUNK]), 0.0)
        h = h + _mm((up * up).astype(BF16), wdown_ref[c * FF_CHUNK:(c + 1) * FF_CHUNK, :])
    y_ref[...] = _rms(h, gfin_ref[...])


def _merge_ffn(x, o, u, gates, hist_args, w, *, tile, n_seq, prompt):
    G, R, _ = x.shape
    nt = R // tile
    row = lambda width: pl.BlockSpec((None, tile, width), lambda g, t: (g, t, 0))
    if prompt:
        (u_meta,) = hist_args
        per_tile = tile // HIST_ROWS
        hist_specs = [
            pl.BlockSpec((None, HIST_ROWS, D_POOL), lambda g, t: (g, jnp.maximum(t * per_tile - 1, 0), 0)),
            pl.BlockSpec((HIST_ROWS, D_POOL), lambda g, t: (0, 0)),
        ]
        hist_in = [u, u_meta]
    else:
        (hist,) = hist_args
        hist_specs = [pl.BlockSpec((n_seq, HIST_ROWS, D_POOL), lambda g, t: (t, 0, 0))]
        hist_in = [hist]
    weights = (w["w_attn_br"], w["w_pool_grp"], w["pool_scale"], w["w_pool_br"], w["w_out"],
               w["g_ffn"], w["w_up"], w["w_down"], w["g_final"])
    in_specs = ([row(D_MODEL), row(ATTN_WIDTH), row(D_POOL)] + hist_specs + [row(2 * D_MODEL)]
                + [_const_spec(a.shape) for a in weights])
    return pl.pallas_call(
        functools.partial(_merge_body, n_seq=n_seq, prompt=prompt),
        grid=(G, nt),
        in_specs=in_specs,
        out_specs=row(D_MODEL),
        out_shape=jax.ShapeDtypeStruct((G, R, D_MODEL), F32),
        compiler_params=pltpu.CompilerParams(dimension_semantics=("parallel", "parallel"),
                                             vmem_limit_bytes=VMEM_LIMIT),
        name="merge_ffn",
    )(x, o, u, *hist_in, gates, *weights)


def _rope_tables(pos):
    half = QK_ROPE // 2
    inv_freq = jnp.exp(-math.log(ROPE_BASE) * jnp.arange(half, dtype=jnp.float32) / half)
    ang = pos.astype(jnp.float32)[:, None] * inv_freq[None, :]
    c, s = jnp.cos(ang), jnp.sin(ang)
    reps = LANES // QK_ROPE
    return jnp.concatenate([c, c] * reps, axis=-1), jnp.concatenate([-s, s] * reps, axis=-1)


def _prep_weights(l, w_in, g_norm_mix, g_q, g_kv, w_q_up, w_uk, w_uv, w_attn_br, w_pool_grp,
                  pool_scale, w_pool_br, w_out, g_norm_ffn, w_up, w_down, g_final):
    half = QK_ROPE // 2
    i0, i1, i2, i3 = Q_LORA, Q_LORA + KV_LORA, Q_LORA + KV_LORA + QK_ROPE, Q_LORA + KV_LORA + QK_ROPE + D_POOL
    wi = w_in[l]
    w_kr = wi[:, i1:i2]
    w_kr_sw = jnp.concatenate([w_kr[:, half:], w_kr[:, :half]], axis=-1)
    wq = w_q_up[l].reshape(Q_LORA, N_HEADS, QK_NOPE + QK_ROPE)
    wq_r1, wq_r2 = wq[:, :, QK_NOPE:QK_NOPE + half], wq[:, :, QK_NOPE + half:]
    row = lambda v: v.reshape(1, -1).astype(F32)
    return {
        "g_mix": row(g_norm_mix[l]), "g_q": row(g_q[l]), "g_kv": row(g_kv[l]),
        "w_qlat": wi[:, :i0].astype(BF16),
        "w_kvr": jnp.concatenate([wi[:, i0:i1], w_kr, w_kr, w_kr_sw, w_kr_sw], axis=-1).astype(BF16),
        "w_u": wi[:, i2:i3].astype(BF16),
        "w_g": wi[:, i3:].astype(BF16),
        "w_qup": jnp.concatenate([
            wq[:, :, :QK_NOPE].reshape(Q_LORA, -1),
            jnp.concatenate([wq_r1, wq_r2], axis=-1).reshape(Q_LORA, -1),
            jnp.concatenate([wq_r2, wq_r1], axis=-1).reshape(Q_LORA, -1)], axis=-1).astype(BF16),
        "w_uk": w_uk[l].reshape(KV_LORA, N_HEADS * QK_NOPE).astype(BF16),
        "w_uvt": w_uv[l].reshape(KV_LORA, N_HEADS * V_HEAD).T.astype(BF16),
        "w_ukt": jnp.transpose(w_uk[l], (1, 2, 0)).astype(BF16),
        "w_uv3": jnp.transpose(w_uv[l], (1, 0, 2)).astype(BF16),
        "w_attn_br": w_attn_br[l].astype(BF16),
        "w_pool_grp": w_pool_grp[l].astype(BF16),
        "pool_scale": row(pool_scale[l]),
        "w_pool_br": w_pool_br[l].astype(BF16),
        "w_out": w_out[l].astype(BF16),
        "g_ffn": row(g_norm_ffn[l]),
        "w_up": w_up[l].astype(BF16),
        "w_down": w_down[l].astype(BF16),
        "g_final": row(g_final),
    }


def kernel(x_prompt, x_sample, cache_kv_latent, cache_k_rope, cache_pool, meta_tokens, w_in, g_norm_mix, g_q, g_kv, w_q_up, w_uk, w_uv, w_attn_br, w_pool_grp, pool_scale, w_pool_br, w_out, g_norm_ffn, w_up, w_down, g_final):
    B, S, _ = x_prompt.shape
    Bd, T, _ = x_sample.shape
    past = cache_kv_latent.shape[2]
    assert w_in.shape[0] == 1 and S % ROW_TILE == 0 and (Bd * T) % ROW_TILE == 0 and ROW_TILE % T == 0
    assert N_META >= max(POOL_WINDOWS) and POOL_HIST + 1 >= max(POOL_WINDOWS) and T >= POOL_HIST
    w = _prep_weights(0, w_in, g_norm_mix, g_q, g_kv, w_q_up, w_uk, w_uv, w_attn_br, w_pool_grp,
                      pool_scale, w_pool_br, w_out, g_norm_ffn, w_up, w_down, g_final)

    meta = jnp.concatenate([meta_tokens.astype(F32), jnp.zeros((META_PAD - N_META, D_MODEL), F32)], axis=0)
    cos_m, sin_m = _rope_tables(jnp.arange(META_PAD, dtype=jnp.int32))
    pm = _project(meta[None], cos_m, sin_m, w, tile=META_PAD, emit_q=False, emit_kv=True,
                  emit_gates=False, emit_kpe2=False)

    cos_p, sin_p = _rope_tables(N_META + jnp.arange(S, dtype=jnp.int32))
    pp = _project(x_prompt, cos_p, sin_p, w, tile=ROW_TILE, emit_q=True, emit_kv=True,
                  emit_gates=True, emit_kpe2=False)
    o_p = _flash(pp["q"], pp["k"], pp["vt"], pm["k"], pm["vt"])
    y_prompt = _merge_ffn(x_prompt, o_p, pp["u"], pp["gates"], (pm["u"][0],), w,
                          tile=ROW_TILE, n_seq=1, prompt=True)

    rs = Bd * T
    cos_s, sin_s = _rope_tables(jnp.tile(past + jnp.arange(T, dtype=jnp.int32), Bd))
    xs = x_sample.reshape(1, rs, D_MODEL)
    ps = _project(xs, cos_s, sin_s, w, tile=ROW_TILE, emit_q=True, emit_kv=False,
                  emit_gates=True, emit_kpe2=True)
    cache_pe2 = jnp.concatenate([cache_k_rope[0]] * (LANES // QK_ROPE), axis=-1).astype(BF16)
    o_s = _decode(ps["q"][0], ps["c"][0], ps["kpe2"][0], cache_kv_latent[0], cache_pe2,
                  w["w_ukt"], w["w_uv3"], seq=T)
    hist_s = jnp.concatenate([jnp.zeros((Bd, HIST_ROWS - POOL_HIST, D_POOL), F32), cache_pool[0]], axis=1)
    y_sample = _merge_ffn(xs, o_s[None], ps["u"], ps["gates"], (hist_s,), w,
                          tile=ROW_TILE, n_seq=ROW_TILE // T, prompt=False).reshape(Bd, T, D_MODEL)

    def with_meta(m, f):
        return jnp.concatenate([jnp.broadcast_to(m[:, :N_META], (B,) + m[0, :N_META].shape), f], axis=1)[None]

    c_s = ps["c"].reshape(Bd, T, KV_LORA)
    pe_s = ps["kpe"].reshape(Bd, T, QK_ROPE)
    u_s = ps["u"].reshape(Bd, T, D_POOL)
    return (y_prompt, y_sample,
            with_meta(pm["c"], pp["c"]), with_meta(pm["kpe"], pp["kpe"]), pp["u"][:, -POOL_HIST:][None],
            c_s[None], pe_s[None], u_s[:, -POOL_HIST:][None])
```

```python
import functools
import math

import jax
import jax.numpy as jnp
from jax import lax
from jax.experimental import pallas as pl
from jax.experimental.pallas import tpu as pltpu

F32 = jnp.float32
BF16 = jnp.bfloat16

D_MODEL = 1024
N_HEADS = 8
Q_LORA = 384
KV_LORA = 256
QK_NOPE = 128
QK_ROPE = 64
V_HEAD = 128
ATTN_WIDTH = N_HEADS * V_HEAD
POOL_WINDOWS = (2, 4, 8, 16)
POOL_GROUP = 128
D_POOL = len(POOL_WINDOWS) * POOL_GROUP
POOL_HIST = max(POOL_WINDOWS) - 1
HIST_ROWS = POOL_HIST + 1
D_FF = 4 * D_MODEL
N_META = 16
CHUNK = 64
ROPE_BASE = 10000.0
EPS = 1e-6
SM_SCALE = (QK_NOPE + QK_ROPE) ** -0.5
LOG2E = 1.4426950408889634
NEG_INF = -1e30

LANES = 128
HEAD_W = QK_NOPE + LANES
ROW_TILE = 512
META_PAD = 128
Q_STREAMS = 4
FF_CHUNK = 1024
VMEM_LIMIT = 56 * 1024 * 1024


def _nt(a, b):
    return lax.dot_general(a, b, (((1,), (1,)), ((), ())), preferred_element_type=F32)


def _mm(a, b):
    return jnp.dot(a, b, preferred_element_type=F32)


def _rms(x, g):
    return x * lax.rsqrt(jnp.mean(x * x, axis=-1, keepdims=True) + EPS) * g


def _const_spec(shape):
    nd = len(shape)
    return pl.BlockSpec(shape, lambda *_: (0,) * nd, pipeline_mode=pl.Buffered(1))


def _project_body(x_ref, cos_ref, sin_ref, gmix_ref, gq_ref, gkv_ref, wql_ref, wkvr_ref, wu_ref,
                  wg_ref, wqup_ref, wuk_ref, wuvt_ref, *outs, emit_q, emit_kv, emit_gates, emit_kpe2):
    outs = list(outs)
    q_ref = outs.pop(0) if emit_q else None
    k_ref, vt_ref = (outs.pop(0), outs.pop(0)) if emit_kv else (None, None)
    c_ref, kpe_ref, u_ref = outs.pop(0), outs.pop(0), outs.pop(0)
    g_ref = outs.pop(0) if emit_gates else None
    kpe2_ref = outs.pop(0) if emit_kpe2 else None

    rows = x_ref.shape[0]
    xn = _rms(x_ref[...], gmix_ref[...]).astype(BF16)
    cos = cos_ref[...]
    sin = sin_ref[...]
    lane = lax.broadcasted_iota(jnp.int32, (rows, LANES), 1)
    half_masks = (lane < QK_ROPE, lane >= QK_ROPE)

    kvr = _mm(xn, wkvr_ref[...])
    c = _rms(kvr[:, :KV_LORA], gkv_ref[...])
    c_ref[...] = c
    kpe2 = kvr[:, KV_LORA:KV_LORA + LANES] * cos + kvr[:, KV_LORA + LANES:] * sin
    kpe_ref[...] = kpe2[:, :QK_ROPE]
    if emit_kpe2:
        kpe2_ref[...] = kpe2.astype(BF16)
    cb = c.astype(BF16)

    if emit_kv:
        k_nope = _mm(cb, wuk_ref[...])
        for h in range(N_HEADS):
            k_ref[:, h * HEAD_W:h * HEAD_W + QK_NOPE] = k_nope[:, h * QK_NOPE:(h + 1) * QK_NOPE].astype(BF16)
            k_ref[:, h * HEAD_W + QK_NOPE:(h + 1) * HEAD_W] = jnp.where(half_masks[h % 2], kpe2, 0.0).astype(BF16)
        vt_ref[...] = _nt(wuvt_ref[...], cb).astype(BF16)

    u_ref[...] = _mm(xn, wu_ref[...])

    if emit_gates:
        logits = _mm(xn, wg_ref[...])
        g_ref[...] = (1.0 / (1.0 + jnp.exp(-logits))).astype(BF16)

    if emit_q:
        qn = _rms(_mm(xn, wql_ref[...]), gq_ref[...]).astype(BF16)
        qall = _mm(qn, wqup_ref[...]) * (SM_SCALE * LOG2E)
        nope_w = N_HEADS * QK_NOPE
        rope_w = N_HEADS * QK_ROPE
        for h in range(N_HEADS):
            j = h // 2
            r = qall[:, nope_w + j * LANES:nope_w + (j + 1) * LANES] * cos
            r = r + qall[:, nope_w + rope_w + j * LANES:nope_w + rope_w + (j + 1) * LANES] * sin
            q_ref[:, h * HEAD_W:h * HEAD_W + QK_NOPE] = qall[:, h * QK_NOPE:(h + 1) * QK_NOPE].astype(BF16)
            q_ref[:, h * HEAD_W + QK_NOPE:(h + 1) * HEAD_W] = jnp.where(half_masks[h % 2], r, 0.0).astype(BF16)


def _project(x, cos, sin, w, *, tile, emit_q, emit_kv, emit_gates, emit_kpe2):
    G, R, _ = x.shape
    nt = R // tile
    assert nt * tile == R
    row = lambda width: pl.BlockSpec((None, tile, width), lambda g, t: (g, t, 0))
    tab = pl.BlockSpec((tile, LANES), lambda g, t: (t, 0))
    weights = (w["g_mix"], w["g_q"], w["g_kv"], w["w_qlat"], w["w_kvr"], w["w_u"], w["w_g"],
               w["w_qup"], w["w_uk"], w["w_uvt"])
    in_specs = [row(D_MODEL), tab, tab] + [_const_spec(a.shape) for a in weights]
    out_shape, out_specs = [], []

    def add(shape, spec, dtype):
        out_shape.append(jax.ShapeDtypeStruct(shape, dtype))
        out_specs.append(spec)

    if emit_q:
        add((G, R, N_HEADS * HEAD_W), row(N_HEADS * HEAD_W), BF16)
    if emit_kv:
        add((G, nt, tile, N_HEADS * HEAD_W),
            pl.BlockSpec((None, None, tile, N_HEADS * HEAD_W), lambda g, t: (g, t, 0, 0)), BF16)
        add((G, nt, ATTN_WIDTH, tile),
            pl.BlockSpec((None, None, ATTN_WIDTH, tile), lambda g, t: (g, t, 0, 0)), BF16)
    add((G, R, KV_LORA), row(KV_LORA), F32)
    add((G, R, QK_ROPE), row(QK_ROPE), F32)
    add((G, R, D_POOL), row(D_POOL), F32)
    if emit_gates:
        add((G, R, 2 * D_MODEL), row(2 * D_MODEL), BF16)
    if emit_kpe2:
        add((G, R, LANES), row(LANES), BF16)

    body = functools.partial(_project_body, emit_q=emit_q, emit_kv=emit_kv, emit_gates=emit_gates,
                             emit_kpe2=emit_kpe2)
    outs = pl.pallas_call(
        body,
        grid=(G, nt),
        in_specs=in_specs,
        out_specs=out_specs,
        out_shape=out_shape,
        compiler_params=pltpu.CompilerParams(dimension_semantics=("parallel", "parallel"),
                                             vmem_limit_bytes=VMEM_LIMIT),
        name="project",
    )(x, cos, sin, *weights)
    outs = list(outs)
    res = {}
    if emit_q:
        res["q"] = outs.pop(0)
    if emit_kv:
        res["k"], res["vt"] = outs.pop(0), outs.pop(0)
    res["c"], res["kpe"], res["u"] = outs.pop(0), outs.pop(0), outs.pop(0)
    if emit_gates:
        res["gates"] = outs.pop(0)
    if emit_kpe2:
        res["kpe2"] = outs.pop(0)
    return res


def _flash_body(q_ref, k_ref, vt_ref, km_ref, vtm_ref, o_ref, m_ref, l_ref, acc_ref, s_ref, cm_ref, *, streams):
    tk = k_ref.shape[1]
    base = pl.program_id(2) * streams

    def q_blk(s):
        return q_ref[s * tk:(s + 1) * tk, :]

    def produce(buf, kb, ss):
        for s in ss:
            s_t = _nt(k_ref[kb], q_blk(s))
            s_ref[buf, s] = s_t
            cm_ref[buf, s] = jnp.max(s_t, axis=0, keepdims=True)

    def update(s, s_t, m_blk, vt_blk, first=False):
        if first:
            m_new = m_blk
        else:
            m_old = m_ref[s]
            m_new = jnp.maximum(m_old, m_blk)
        p = jnp.exp2(s_t - m_new)
        l_blk = jnp.sum(p, axis=0, keepdims=True)
        pv = _mm(vt_blk, p.astype(BF16))
        if first:
            l_ref[s] = l_blk
            acc_ref[s] = pv
        else:
            alpha = jnp.exp2(m_old - m_new)
            l_ref[s] = alpha * l_ref[s] + l_blk
            acc_ref[s] = alpha * acc_ref[s] + pv
        m_ref[s] = m_new

    def consume(buf, s, kb, mask=None):
        s_t = s_ref[buf, s]
        if mask is None:
            m_blk = cm_ref[buf, s]
        else:
            s_t = jnp.where(mask, s_t, NEG_INF)
            m_blk = jnp.max(s_t, axis=0, keepdims=True)
        update(s, s_t, m_blk, vt_ref[kb])

    everyone = range(streams)
    produce(0, 0, everyone)

    real = lax.broadcasted_iota(jnp.int32, (META_PAD, tk), 0) < N_META
    for s in everyone:
        s_m = jnp.where(real, _nt(km_ref[...], q_blk(s)), NEG_INF)
        update(s, s_m, jnp.max(s_m, axis=0, keepdims=True), vtm_ref[...], first=True)

    def pair(j, carry):
        kb = 2 * j
        produce(1, kb + 1, everyone)
        for s in everyone:
            consume(0, s, kb)
        produce(0, kb + 2, everyone)
        for s in everyone:
            consume(1, s, kb + 1)
        return carry

    lax.fori_loop(0, base // 2, pair, 0)

    causal = (lax.broadcasted_iota(jnp.int32, (tk, tk), 0) // CHUNK
              <= lax.broadcasted_iota(jnp.int32, (tk, tk), 1) // CHUNK)
    for w in everyone:
        buf = w % 2
        if w + 1 < streams:
            produce(1 - buf, base + w + 1, range(w + 1, streams))
        for s in range(w, streams):
            consume(buf, s, base + w, mask=causal if s == w else None)

    for s in everyone:
        o_ref[s * tk:(s + 1) * tk, :] = (acc_ref[s] / l_ref[s]).T.astype(BF16)


def _flash(q, k, vt, k_meta, vt_meta):
    B, S, _ = q.shape
    nt, tile = k.shape[1], k.shape[2]
    streams = Q_STREAMS
    assert nt % streams == 0 and streams % 2 == 0
    return pl.pallas_call(
        functools.partial(_flash_body, streams=streams),
        grid=(B, N_HEADS, nt // streams),
        in_specs=[
            pl.BlockSpec((None, streams * tile, HEAD_W), lambda b, h, i: (b, i, h)),
            pl.BlockSpec((None, nt, tile, HEAD_W), lambda b, h, i: (b, 0, 0, h)),
            pl.BlockSpec((None, nt, V_HEAD, tile), lambda b, h, i: (b, 0, h, 0)),
            pl.BlockSpec((None, None, META_PAD, HEAD_W), lambda b, h, i: (0, 0, 0, h)),
            pl.BlockSpec((None, None, V_HEAD, META_PAD), lambda b, h, i: (0, 0, h, 0)),
        ],
        out_specs=pl.BlockSpec((None, streams * tile, V_HEAD), lambda b, h, i: (b, i, h)),
        out_shape=jax.ShapeDtypeStruct((B, S, ATTN_WIDTH), BF16),
        scratch_shapes=[pltpu.VMEM((streams, 1, tile), F32), pltpu.VMEM((streams, 1, tile), F32),
                        pltpu.VMEM((streams, V_HEAD, tile), F32),
                        pltpu.VMEM((2, streams, tile, tile), F32), pltpu.VMEM((2, streams, 1, tile), F32)],
        compiler_params=pltpu.CompilerParams(dimension_semantics=("parallel", "parallel", "arbitrary"),
                                             vmem_limit_bytes=VMEM_LIMIT),
        name="flash",
    )(q, k, vt, k_meta, vt_meta)


def _decode_body(q_ref, cn_ref, pn_ref, cc_ref, pc_ref, wukt_ref, wuv_ref, o_ref):
    q = q_ref[...]
    t = q.shape[0]
    qa = jnp.concatenate(
        [_mm(q[:, h * HEAD_W:h * HEAD_W + QK_NOPE], wukt_ref[h]) for h in range(N_HEADS)], axis=0).astype(BF16)
    qp = jnp.concatenate([q[:, h * HEAD_W + QK_NOPE:(h + 1) * HEAD_W] for h in range(N_HEADS)], axis=0)
    cc = cc_ref[...].astype(BF16)
    cn = cn_ref[...].astype(BF16)
    s_c = _nt(qa, cc) + _nt(qp, pc_ref[...])
    s_n = _nt(qa, cn) + _nt(qp, pn_ref[...])
    m = jnp.maximum(jnp.max(s_c, axis=-1, keepdims=True), jnp.max(s_n, axis=-1, keepdims=True))
    p_c = jnp.exp2(s_c - m)
    p_n = jnp.exp2(s_n - m)
    l = jnp.sum(p_c, axis=-1, keepdims=True) + jnp.sum(p_n, axis=-1, keepdims=True)
    o_lat = ((_mm(p_c.astype(BF16), cc) + _mm(p_n.astype(BF16), cn)) / l).astype(BF16)
    for h in range(N_HEADS):
        o_ref[:, h * V_HEAD:(h + 1) * V_HEAD] = _mm(o_lat[h * t:(h + 1) * t], wuv_ref[h]).astype(BF16)


def _decode(q, c_new, kpe2_new, cache_c, cache_pe2, wukt, wuv, *, seq):
    R = q.shape[0]
    nb, past, _ = cache_c.shape
    assert nb * seq == R
    return pl.pallas_call(
        _decode_body,
        grid=(nb,),
        in_specs=[
            pl.BlockSpec((seq, N_HEADS * HEAD_W), lambda i: (i, 0)),
            pl.BlockSpec((seq, KV_LORA), lambda i: (i, 0)),
            pl.BlockSpec((seq, LANES), lambda i: (i, 0)),
            pl.BlockSpec((None, past, KV_LORA), lambda i: (i, 0, 0)),
            pl.BlockSpec((None, past, LANES), lambda i: (i, 0, 0)),
            _const_spec(wukt.shape),
            _const_spec(wuv.shape),
        ],
        out_specs=pl.BlockSpec((seq, ATTN_WIDTH), lambda i: (i, 0)),
        out_shape=jax.ShapeDtypeStruct((R, ATTN_WIDTH), BF16),
        compiler_params=pltpu.CompilerParams(dimension_semantics=("parallel",),
                                             vmem_limit_bytes=VMEM_LIMIT),
        name="decode",
    )(q, c_new, kpe2_new, cache_c, cache_pe2, wukt, wuv)


def _pool_seq(hist, u):
    ext = jnp.concatenate([hist, u], axis=0)
    outs = []
    for g, w in enumerate(POOL_WINDOWS):
        e = ext[:, g * POOL_GROUP:(g + 1) * POOL_GROUP]
        s = e
        shift = 1
        while shift < w:
            s = s + pltpu.roll(s, shift, axis=0)
            shift *= 2
        outs.append(s[HIST_ROWS:] * (1.0 / w) - e[HIST_ROWS:])
    return outs


def _merge_body(*refs, n_seq, prompt):
    if prompt:
        (x_ref, o_ref, u_ref, hprev_ref, hmeta_ref, g_ref, wab_ref, wpg_ref, ps_ref, wpb_ref, wout_ref,
         gffn_ref, wup_ref, wdown_ref, gfin_ref, y_ref) = refs
        first = pl.program_id(1) == 0
        hists = [jnp.where(first, hmeta_ref[...], hprev_ref[...])]
    else:
        (x_ref, o_ref, u_ref, hist_ref, g_ref, wab_ref, wpg_ref, ps_ref, wpb_ref, wout_ref,
         gffn_ref, wup_ref, wdown_ref, gfin_ref, y_ref) = refs
        hists = [hist_ref[i] for i in range(n_seq)]

    rows = x_ref.shape[0]
    t = rows // n_seq
    u = u_ref[...]
    per_seq = [_pool_seq(hists[i], u[i * t:(i + 1) * t]) for i in range(n_seq)]
    mixed = []
    for g in range(len(POOL_WINDOWS)):
        pooled = jnp.concatenate([per_seq[i][g] for i in range(n_seq)], axis=0) if n_seq > 1 else per_seq[0][g]
        mixed.append(_mm(pooled.astype(BF16), wpg_ref[g]))
    pm = (jnp.concatenate(mixed, axis=-1) * ps_ref[...]).astype(BF16)
    p = _mm(pm, wpb_ref[...])
    a = _mm(o_ref[...], wab_ref[...])
    gates = g_ref[...].astype(F32)
    mix = (gates[:, :D_MODEL] * a + gates[:, D_MODEL:] * p).astype(BF16)
    h = x_ref[...] + _mm(mix, wout_ref[...])
    hn = _rms(h, gffn_ref[...]).astype(BF16)
    for c in range(D_FF // FF_CHUNK):
        up = jnp.maximum(_mm(hn, wup_ref[:, c * FF_CHUNK:(c + 1) * FF_CHUNK]), 0.0)
        h = h + _mm((up * up).astype(BF16), wdown_ref[c * FF_CHUNK:(c + 1) * FF_CHUNK, :])
    y_ref[...] = _rms(h, gfin_ref[...])


def _merge_ffn(x, o, u, gates, hist_args, w, *, tile, n_seq, prompt):
    G, R, _ = x.shape
    nt = R // tile
    row = lambda width: pl.BlockSpec((None, tile, width), lambda g, t: (g, t, 0))
    if prompt:
        (u_meta,) = hist_args
        per_tile = tile // HIST_ROWS
        hist_specs = [
            pl.BlockSpec((None, HIST_ROWS, D_POOL), lambda g, t: (g, jnp.maximum(t * per_tile - 1, 0), 0)),
            pl.BlockSpec((HIST_ROWS, D_POOL), lambda g, t: (0, 0)),
        ]
        hist_in = [u, u_meta]
    else:
        (hist,) = hist_args
        hist_specs = [pl.BlockSpec((n_seq, HIST_ROWS, D_POOL), lambda g, t: (t, 0, 0))]
        hist_in = [hist]
    weights = (w["w_attn_br"], w["w_pool_grp"], w["pool_scale"], w["w_pool_br"], w["w_out"],
               w["g_ffn"], w["w_up"], w["w_down"], w["g_final"])
    in_specs = ([row(D_MODEL), row(ATTN_WIDTH), row(D_POOL)] + hist_specs + [row(2 * D_MODEL)]
                + [_const_spec(a.shape) for a in weights])
    return pl.pallas_call(
        functools.partial(_merge_body, n_seq=n_seq, prompt=prompt),
        grid=(G, nt),
        in_specs=in_specs,
        out_specs=row(D_MODEL),
        out_shape=jax.ShapeDtypeStruct((G, R, D_MODEL), F32),
        compiler_params=pltpu.CompilerParams(dimension_semantics=("parallel", "parallel"),
                                             vmem_limit_bytes=VMEM_LIMIT),
        name="merge_ffn",
    )(x, o, u, *hist_in, gates, *weights)


def _rope_tables(pos):
    half = QK_ROPE // 2
    inv_freq = jnp.exp(-math.log(ROPE_BASE) * jnp.arange(half, dtype=jnp.float32) / half)
    ang = pos.astype(jnp.float32)[:, None] * inv_freq[None, :]
    c, s = jnp.cos(ang), jnp.sin(ang)
    reps = LANES // QK_ROPE
    return jnp.concatenate([c, c] * reps, axis=-1), jnp.concatenate([-s, s] * reps, axis=-1)


def _prep_weights(l, w_in, g_norm_mix, g_q, g_kv, w_q_up, w_uk, w_uv, w_attn_br, w_pool_grp,
                  pool_scale, w_pool_br, w_out, g_norm_ffn, w_up, w_down, g_final):
    half = QK_ROPE // 2
    i0, i1, i2, i3 = Q_LORA, Q_LORA + KV_LORA, Q_LORA + KV_LORA + QK_ROPE, Q_LORA + KV_LORA + QK_ROPE + D_POOL
    wi = w_in[l]
    w_kr = wi[:, i1:i2]
    w_kr_sw = jnp.concatenate([w_kr[:, half:], w_kr[:, :half]], axis=-1)
    wq = w_q_up[l].reshape(Q_LORA, N_HEADS, QK_NOPE + QK_ROPE)
    wq_r1, wq_r2 = wq[:, :, QK_NOPE:QK_NOPE + half], wq[:, :, QK_NOPE + half:]
    row = lambda v: v.reshape(1, -1).astype(F32)
    return {
        "g_mix": row(g_norm_mix[l]), "g_q": row(g_q[l]), "g_kv": row(g_kv[l]),
        "w_qlat": wi[:, :i0].astype(BF16),
        "w_kvr": jnp.concatenate([wi[:, i0:i1], w_kr, w_kr, w_kr_sw, w_kr_sw], axis=-1).astype(BF16),
        "w_u": wi[:, i2:i3].astype(BF16),
        "w_g": wi[:, i3:].astype(BF16),
        "w_qup": jnp.concatenate([
            wq[:, :, :QK_NOPE].reshape(Q_LORA, -1),
            jnp.concatenate([wq_r1, wq_r2], axis=-1).reshape(Q_LORA, -1),
            jnp.concatenate([wq_r2, wq_r1], axis=-1).reshape(Q_LORA, -1)], axis=-1).astype(BF16),
        "w_uk": w_uk[l].reshape(KV_LORA, N_HEADS * QK_NOPE).astype(BF16),
        "w_uvt": w_uv[l].reshape(KV_LORA, N_HEADS * V_HEAD).T.astype(BF16),
        "w_ukt": jnp.transpose(w_uk[l], (1, 2, 0)).astype(BF16),
        "w_uv3": jnp.transpose(w_uv[l], (1, 0, 2)).astype(BF16),
        "w_attn_br": w_attn_br[l].astype(BF16),
        "w_pool_grp": w_pool_grp[l].astype(BF16),
        "pool_scale": row(pool_scale[l]),
        "w_pool_br": w_pool_br[l].astype(BF16),
        "w_out": w_out[l].astype(BF16),
        "g_ffn": row(g_norm_ffn[l]),
        "w_up": w_up[l].astype(BF16),
        "w_down": w_down[l].astype(BF16),
        "g_final": row(g_final),
    }


def kernel(x_prompt, x_sample, cache_kv_latent, cache_k_rope, cache_pool, meta_tokens, w_in, g_norm_mix, g_q, g_kv, w_q_up, w_uk, w_uv, w_attn_br, w_pool_grp, pool_scale, w_pool_br, w_out, g_norm_ffn, w_up, w_down, g_final):
    B, S, _ = x_prompt.shape
    Bd, T, _ = x_sample.shape
    past = cache_kv_latent.shape[2]
    assert w_in.shape[0] == 1 and S % ROW_TILE == 0 and (Bd * T) % ROW_TILE == 0 and ROW_TILE % T == 0
    assert N_META >= max(POOL_WINDOWS) and POOL_HIST + 1 >= max(POOL_WINDOWS) and T >= POOL_HIST
    w = _prep_weights(0, w_in, g_norm_mix, g_q, g_kv, w_q_up, w_uk, w_uv, w_attn_br, w_pool_grp,
                      pool_scale, w_pool_br, w_out, g_norm_ffn, w_up, w_down, g_final)

    meta = jnp.concatenate([meta_tokens.astype(F32), jnp.zeros((META_PAD - N_META, D_MODEL), F32)], axis=0)
    cos_m, sin_m = _rope_tables(jnp.arange(META_PAD, dtype=jnp.int32))
    pm = _project(meta[None], cos_m, sin_m, w, tile=META_PAD, emit_q=False, emit_kv=True,
                  emit_gates=False, emit_kpe2=False)

    cos_p, sin_p = _rope_tables(N_META + jnp.arange(S, dtype=jnp.int32))
    pp = _project(x_prompt, cos_p, sin_p, w, tile=ROW_TILE, emit_q=True, emit_kv=True,
                  emit_gates=True, emit_kpe2=False)
    o_p = _flash(pp["q"], pp["k"], pp["vt"], pm["k"], pm["vt"])
    y_prompt = _merge_ffn(x_prompt, o_p, pp["u"], pp["gates"], (pm["u"][0],), w,
                          tile=ROW_TILE, n_seq=1, prompt=True)

    rs = Bd * T
    cos_s, sin_s = _rope_tables(jnp.tile(past + jnp.arange(T, dtype=jnp.int32), Bd))
    xs = x_sample.reshape(1, rs, D_MODEL)
    ps = _project(xs, cos_s, sin_s, w, tile=ROW_TILE, emit_q=True, emit_kv=False,
                  emit_gates=True, emit_kpe2=True)
    cache_pe2 = jnp.concatenate([cache_k_rope[0]] * (LANES // QK_ROPE), axis=-1).astype(BF16)
    o_s = _decode(ps["q"][0], ps["c"][0], ps["kpe2"][0], cache_kv_latent[0], cache_pe2,
                  w["w_ukt"], w["w_uv3"], seq=T)
    hist_s = jnp.concatenate([jnp.zeros((Bd, HIST_ROWS - POOL_HIST, D_POOL), F32), cache_pool[0]], axis=1)
    y_sample = _merge_ffn(xs, o_s[None], ps["u"], ps["gates"], (hist_s,), w,
                          tile=ROW_TILE, n_seq=ROW_TILE // T, prompt=False).reshape(Bd, T, D_MODEL)

    def with_meta(m, f):
        return jnp.concatenate([jnp.broadcast_to(m[:, :N_META], (B,) + m[0, :N_META].shape), f], axis=1)[None]

    c_s = ps["c"].reshape(Bd, T, KV_LORA)
    pe_s = ps["kpe"].reshape(Bd, T, QK_ROPE)
    u_s = ps["u"].reshape(Bd, T, D_POOL)
    return (y_prompt, y_sample,
            with_meta(pm["c"], pp["c"]), with_meta(pm["kpe"], pp["kpe"]), pp["u"][:, -POOL_HIST:][None],
            c_s[None], pe_s[None], u_s[:, -POOL_HIST:][None])
```

```python
import functools
import math

import jax
import jax.numpy as jnp
from jax import lax
from jax.experimental import pallas as pl
from jax.experimental.pallas import tpu as pltpu

F32 = jnp.float32
BF16 = jnp.bfloat16

D_MODEL = 1024
N_HEADS = 8
Q_LORA = 384
KV_LORA = 256
QK_NOPE = 128
QK_ROPE = 64
V_HEAD = 128
ATTN_WIDTH = N_HEADS * V_HEAD
POOL_WINDOWS = (2, 4, 8, 16)
POOL_GROUP = 128
D_POOL = len(POOL_WINDOWS) * POOL_GROUP
POOL_HIST = max(POOL_WINDOWS) - 1
HIST_ROWS = POOL_HIST + 1
D_FF = 4 * D_MODEL
N_META = 16
CHUNK = 64
ROPE_BASE = 10000.0
EPS = 1e-6
SM_SCALE = (QK_NOPE + QK_ROPE) ** -0.5
LOG2E = 1.4426950408889634
NEG_INF = -1e30

LANES = 128
HEAD_W = QK_NOPE + LANES
V_ROWS = V_HEAD + 16
ROW_TILE = 512
META_PAD = 128
Q_STREAMS = 4
FF_CHUNK = 1024
VMEM_LIMIT = 56 * 1024 * 1024


def _nt(a, b):
    return lax.dot_general(a, b, (((1,), (1,)), ((), ())), preferred_element_type=F32)


def _mm(a, b):
    return jnp.dot(a, b, preferred_element_type=F32)


def _rms(x, g):
    return x * lax.rsqrt(jnp.mean(x * x, axis=-1, keepdims=True) + EPS) * g


def _const_spec(shape):
    nd = len(shape)
    return pl.BlockSpec(shape, lambda *_: (0,) * nd, pipeline_mode=pl.Buffered(1))


def _project_body(x_ref, cos_ref, sin_ref, gmix_ref, gq_ref, gkv_ref, wql_ref, wkvr_ref, wu_ref,
                  wg_ref, wqup_ref, wuk_ref, wuvt_ref, *outs, emit_q, emit_kv, emit_gates, emit_kpe2):
    outs = list(outs)
    q_ref = outs.pop(0) if emit_q else None
    k_ref, vt_ref = (outs.pop(0), outs.pop(0)) if emit_kv else (None, None)
    c_ref, kpe_ref, u_ref = outs.pop(0), outs.pop(0), outs.pop(0)
    g_ref = outs.pop(0) if emit_gates else None
    kpe2_ref = outs.pop(0) if emit_kpe2 else None

    rows = x_ref.shape[0]
    xn = _rms(x_ref[...], gmix_ref[...]).astype(BF16)
    cos = cos_ref[...]
    sin = sin_ref[...]
    lane = lax.broadcasted_iota(jnp.int32, (rows, LANES), 1)
    half_masks = (lane < QK_ROPE, lane >= QK_ROPE)

    kvr = _mm(xn, wkvr_ref[...])
    c = _rms(kvr[:, :KV_LORA], gkv_ref[...])
    c_ref[...] = c
    kpe2 = kvr[:, KV_LORA:KV_LORA + LANES] * cos + kvr[:, KV_LORA + LANES:] * sin
    kpe_ref[...] = kpe2[:, :QK_ROPE]
    if emit_kpe2:
        kpe2_ref[...] = kpe2.astype(BF16)
    cb = c.astype(BF16)

    if emit_kv:
        k_nope = _mm(cb, wuk_ref[...])
        for h in range(N_HEADS):
            k_ref[:, h * HEAD_W:h * HEAD_W + QK_NOPE] = k_nope[:, h * QK_NOPE:(h + 1) * QK_NOPE].astype(BF16)
            k_ref[:, h * HEAD_W + QK_NOPE:(h + 1) * HEAD_W] = jnp.where(half_masks[h % 2], kpe2, 0.0).astype(BF16)
        vt = _nt(wuvt_ref[...], cb)
        extra = V_ROWS - V_HEAD
        ones_rows = jnp.where(lax.broadcasted_iota(jnp.int32, (extra, rows), 0) == 0, 1.0, 0.0).astype(BF16)
        for h in range(N_HEADS):
            vt_ref[h * V_ROWS:h * V_ROWS + V_HEAD, :] = vt[h * V_HEAD:(h + 1) * V_HEAD].astype(BF16)
            vt_ref[h * V_ROWS + V_HEAD:(h + 1) * V_ROWS, :] = ones_rows

    u_ref[...] = _mm(xn, wu_ref[...])

    if emit_gates:
        logits = _mm(xn, wg_ref[...])
        g_ref[...] = (1.0 / (1.0 + jnp.exp(-logits))).astype(BF16)

    if emit_q:
        qn = _rms(_mm(xn, wql_ref[...]), gq_ref[...]).astype(BF16)
        qall = _mm(qn, wqup_ref[...]) * (SM_SCALE * LOG2E)
        nope_w = N_HEADS * QK_NOPE
        rope_w = N_HEADS * QK_ROPE
        for h in range(N_HEADS):
            j = h // 2
            r = qall[:, nope_w + j * LANES:nope_w + (j + 1) * LANES] * cos
            r = r + qall[:, nope_w + rope_w + j * LANES:nope_w + rope_w + (j + 1) * LANES] * sin
            q_ref[:, h * HEAD_W:h * HEAD_W + QK_NOPE] = qall[:, h * QK_NOPE:(h + 1) * QK_NOPE].astype(BF16)
            q_ref[:, h * HEAD_W + QK_NOPE:(h + 1) * HEAD_W] = jnp.where(half_masks[h % 2], r, 0.0).astype(BF16)


def _project(x, cos, sin, w, *, tile, emit_q, emit_kv, emit_gates, emit_kpe2):
    G, R, _ = x.shape
    nt = R // tile
    assert nt * tile == R
    row = lambda width: pl.BlockSpec((None, tile, width), lambda g, t: (g, t, 0))
    tab = pl.BlockSpec((tile, LANES), lambda g, t: (t, 0))
    weights = (w["g_mix"], w["g_q"], w["g_kv"], w["w_qlat"], w["w_kvr"], w["w_u"], w["w_g"],
               w["w_qup"], w["w_uk"], w["w_uvt"])
    in_specs = [row(D_MODEL), tab, tab] + [_const_spec(a.shape) for a in weights]
    out_shape, out_specs = [], []

    def add(shape, spec, dtype):
        out_shape.append(jax.ShapeDtypeStruct(shape, dtype))
        out_specs.append(spec)

    if emit_q:
        add((G, R, N_HEADS * HEAD_W), row(N_HEADS * HEAD_W), BF16)
    if emit_kv:
        add((G, nt, tile, N_HEADS * HEAD_W),
            pl.BlockSpec((None, None, tile, N_HEADS * HEAD_W), lambda g, t: (g, t, 0, 0)), BF16)
        add((G, nt, N_HEADS * V_ROWS, tile),
            pl.BlockSpec((None, None, N_HEADS * V_ROWS, tile), lambda g, t: (g, t, 0, 0)), BF16)
    add((G, R, KV_LORA), row(KV_LORA), F32)
    add((G, R, QK_ROPE), row(QK_ROPE), F32)
    add((G, R, D_POOL), row(D_POOL), F32)
    if emit_gates:
        add((G, R, 2 * D_MODEL), row(2 * D_MODEL), BF16)
    if emit_kpe2:
        add((G, R, LANES), row(LANES), BF16)

    body = functools.partial(_project_body, emit_q=emit_q, emit_kv=emit_kv, emit_gates=emit_gates,
                             emit_kpe2=emit_kpe2)
    outs = pl.pallas_call(
        body,
        grid=(G, nt),
        in_specs=in_specs,
        out_specs=out_specs,
        out_shape=out_shape,
        compiler_params=pltpu.CompilerParams(dimension_semantics=("parallel", "parallel"),
                                             vmem_limit_bytes=VMEM_LIMIT),
        name="project",
    )(x, cos, sin, *weights)
    outs = list(outs)
    res = {}
    if emit_q:
        res["q"] = outs.pop(0)
    if emit_kv:
        res["k"], res["vt"] = outs.pop(0), outs.pop(0)
    res["c"], res["kpe"], res["u"] = outs.pop(0), outs.pop(0), outs.pop(0)
    if emit_gates:
        res["gates"] = outs.pop(0)
    if emit_kpe2:
        res["kpe2"] = outs.pop(0)
    return res


def _flash_body(q_ref, k_ref, vt_ref, km_ref, vtm_ref, o_ref, m_ref, acc_ref, s_ref, cm_ref, *, streams):
    tk = k_ref.shape[1]
    base = pl.program_id(2) * streams

    def q_blk(s):
        return q_ref[s * tk:(s + 1) * tk, :]

    def produce(buf, kb, ss):
        for s in ss:
            s_t = _nt(k_ref[kb], q_blk(s))
            s_ref[buf, s] = s_t
            cm_ref[buf, s] = jnp.max(s_t, axis=0, keepdims=True)

    def own_block(buf, s):
        n = tk // LANES
        diag = (lax.broadcasted_iota(jnp.int32, (LANES, LANES), 0) // CHUNK
                <= lax.broadcasted_iota(jnp.int32, (LANES, LANES), 1) // CHUNK)
        hidden = jnp.full((LANES, LANES), NEG_INF, F32)
        rows = []
        for i in range(n):
            cols = []
            for j in range(n):
                if i > j:
                    cols.append(hidden)
                    continue
                t = s_ref[buf, s, i * LANES:(i + 1) * LANES, j * LANES:(j + 1) * LANES]
                cols.append(t if i < j else jnp.where(diag, t, NEG_INF))
            rows.append(jnp.concatenate(cols, axis=1))
        return jnp.concatenate(rows, axis=0)

    def consume(buf, s, kb, own=False):
        if own:
            s_t = own_block(buf, s)
            m_blk = jnp.max(s_t, axis=0, keepdims=True)
        else:
            s_t = s_ref[buf, s]
            m_blk = cm_ref[buf, s]
        m_old = m_ref[s]
        m_new = jnp.maximum(m_old, m_blk)
        p = jnp.exp2(s_t - m_new)
        acc_ref[s] = jnp.exp2(m_old - m_new) * acc_ref[s] + _mm(vt_ref[kb], p.astype(BF16))
        m_ref[s] = m_new

    everyone = range(streams)

    s_m = _nt(km_ref[...], q_ref[...])
    produce(0, 0, everyone)
    s_m = jnp.where(lax.broadcasted_iota(jnp.int32, s_m.shape, 0) < N_META, s_m, NEG_INF)
    m_0 = jnp.max(s_m, axis=0, keepdims=True)
    acc_0 = _mm(vtm_ref[...], jnp.exp2(s_m - m_0).astype(BF16))
    for s in everyone:
        m_ref[s] = m_0[:, s * tk:(s + 1) * tk]
        acc_ref[s] = acc_0[:, s * tk:(s + 1) * tk]

    def step(rd, wr, kb):
        lead = 1
        for i in range(streams + lead):
            if i < streams:
                produce(wr, kb + 1, [i])
            if i >= lead:
                consume(rd, i - lead, kb)

    per_trip = 4 if streams % 4 == 0 else 2

    def trip(j, carry):
        for i in range(per_trip):
            step(i % 2, 1 - i % 2, per_trip * j + i)
        return carry

    lax.fori_loop(0, base // per_trip, trip, 0)

    for w in everyone:
        buf = w % 2
        if w + 1 < streams:
            produce(1 - buf, base + w + 1, range(w + 1, streams))
        for s in range(w, streams):
            consume(buf, s, base + w, own=(s == w))

    for s in everyone:
        acc = acc_ref[s]
        o_ref[s * tk:(s + 1) * tk, :] = (acc[:V_HEAD] / acc[V_HEAD:V_HEAD + 1]).T.astype(BF16)


def _flash(q, k, vt, k_meta, vt_meta):
    B, S, _ = q.shape
    nt, tile = k.shape[1], k.shape[2]
    streams = Q_STREAMS
    assert nt % streams == 0 and streams % 2 == 0
    return pl.pallas_call(
        functools.partial(_flash_body, streams=streams),
        grid=(B, N_HEADS, nt // streams),
        in_specs=[
            pl.BlockSpec((None, streams * tile, HEAD_W), lambda b, h, i: (b, i, h)),
            pl.BlockSpec((None, nt, tile, HEAD_W), lambda b, h, i: (b, 0, 0, h)),
            pl.BlockSpec((None, nt, V_ROWS, tile), lambda b, h, i: (b, 0, h, 0)),
            pl.BlockSpec((None, None, META_PAD, HEAD_W), lambda b, h, i: (0, 0, 0, h)),
            pl.BlockSpec((None, None, V_ROWS, META_PAD), lambda b, h, i: (0, 0, h, 0)),
        ],
        out_specs=pl.BlockSpec((None, streams * tile, V_HEAD), lambda b, h, i: (b, i, h)),
        out_shape=jax.ShapeDtypeStruct((B, S, ATTN_WIDTH), BF16),
        scratch_shapes=[pltpu.VMEM((streams, 1, tile), F32),
                        pltpu.VMEM((streams, V_ROWS, tile), F32),
                        pltpu.VMEM((2, streams, tile, tile), F32), pltpu.VMEM((2, streams, 1, tile), F32)],
        compiler_params=pltpu.CompilerParams(dimension_semantics=("parallel", "parallel", "arbitrary"),
                                             vmem_limit_bytes=VMEM_LIMIT),
        name="flash",
    )(q, k, vt, k_meta, vt_meta)


def _decode_body(q_ref, cn_ref, pn_ref, cc_ref, pc_ref, wukt_ref, wuv_ref, o_ref):
    q = q_ref[...]
    t = q.shape[0]
    qa = jnp.concatenate(
        [_mm(q[:, h * HEAD_W:h * HEAD_W + QK_NOPE], wukt_ref[h]) for h in range(N_HEADS)], axis=0).astype(BF16)
    qp = jnp.concatenate([q[:, h * HEAD_W + QK_NOPE:(h + 1) * HEAD_W] for h in range(N_HEADS)], axis=0)
    cc = cc_ref[...].astype(BF16)
    cn = cn_ref[...].astype(BF16)
    s_c = _nt(qa, cc) + _nt(qp, pc_ref[...])
    s_n = _nt(qa, cn) + _nt(qp, pn_ref[...])
    m = jnp.maximum(jnp.max(s_c, axis=-1, keepdims=True), jnp.max(s_n, axis=-1, keepdims=True))
    p_c = jnp.exp2(s_c - m)
    p_n = jnp.exp2(s_n - m)
    l = jnp.sum(p_c, axis=-1, keepdims=True) + jnp.sum(p_n, axis=-1, keepdims=True)
    o_lat = ((_mm(p_c.astype(BF16), cc) + _mm(p_n.astype(BF16), cn)) / l).astype(BF16)
    for h in range(N_HEADS):
        o_ref[:, h * V_HEAD:(h + 1) * V_HEAD] = _mm(o_lat[h * t:(h + 1) * t], wuv_ref[h]).astype(BF16)


def _decode(q, c_new, kpe2_new, cache_c, cache_pe2, wukt, wuv, *, seq):
    R = q.shape[0]
    nb, past, _ = cache_c.shape
    assert nb * seq == R
    return pl.pallas_call(
        _decode_body,
        grid=(nb,),
        in_specs=[
            pl.BlockSpec((seq, N_HEADS * HEAD_W), lambda i: (i, 0)),
            pl.BlockSpec((seq, KV_LORA), lambda i: (i, 0)),
            pl.BlockSpec((seq, LANES), lambda i: (i, 0)),
            pl.BlockSpec((None, past, KV_LORA), lambda i: (i, 0, 0)),
            pl.BlockSpec((None, past, LANES), lambda i: (i, 0, 0)),
            _const_spec(wukt.shape),
            _const_spec(wuv.shape),
        ],
        out_specs=pl.BlockSpec((seq, ATTN_WIDTH), lambda i: (i, 0)),
        out_shape=jax.ShapeDtypeStruct((R, ATTN_WIDTH), BF16),
        compiler_params=pltpu.CompilerParams(dimension_semantics=("parallel",),
                                             vmem_limit_bytes=VMEM_LIMIT),
        name="decode",
    )(q, c_new, kpe2_new, cache_c, cache_pe2, wukt, wuv)


def _pool_seq(hist, u):
    ext = jnp.concatenate([hist, u], axis=0)
    outs = []
    for g, w in enumerate(POOL_WINDOWS):
        e = ext[:, g * POOL_GROUP:(g + 1) * POOL_GROUP]
        s = e
        shift = 1
        while shift < w:
            s = s + pltpu.roll(s, shift, axis=0)
            shift *= 2
        outs.append(s[HIST_ROWS:] * (1.0 / w) - e[HIST_ROWS:])
    return outs


def _merge_body(*refs, n_seq, prompt):
    if prompt:
        (x_ref, o_ref, u_ref, hprev_ref, hmeta_ref, g_ref, wab_ref, wpg_ref, ps_ref, wpb_ref, wout_ref,
         gffn_ref, wup_ref, wdown_ref, gfin_ref, y_ref) = refs
        first = pl.program_id(1) == 0
        hists = [jnp.where(first, hmeta_ref[...], hprev_ref[...])]
    else:
        (x_ref, o_ref, u_ref, hist_ref, g_ref, wab_ref, wpg_ref, ps_ref, wpb_ref, wout_ref,
         gffn_ref, wup_ref, wdown_ref, gfin_ref, y_ref) = refs
        hists = [hist_ref[i] for i in range(n_seq)]

    rows = x_ref.shape[0]
    t = rows // n_seq
    u = u_ref[...]
    per_seq = [_pool_seq(hists[i], u[i * t:(i + 1) * t]) for i in range(n_seq)]
    mixed = []
    for g in range(len(POOL_WINDOWS)):
        pooled = jnp.concatenate([per_seq[i][g] for i in range(n_seq)], axis=0) if n_seq > 1 else per_seq[0][g]
        mixed.append(_mm(pooled.astype(BF16), wpg_ref[g]))
    pm = (jnp.concatenate(mixed, axis=-1) * ps_ref[...]).astype(BF16)
    p = _mm(pm, wpb_ref[...])
    a = _mm(o_ref[...], wab_ref[...])
    gates = g_ref[...].astype(F32)
    mix = (gates[:, :D_MODEL] * a + gates[:, D_MODEL:] * p).astype(BF16)
    h = x_ref[...] + _mm(mix, wout_ref[...])
    hn = _rms(h, gffn_ref[...]).astype(BF16)
    for c in range(D_FF // FF_CHUNK):
        up = jnp.maximum(_mm(hn, wup_ref[:, c * FF_CHUNK:(c + 1) * FF_CHUNK]), 0.0)
        h = h + _mm((up * up).astype(BF16), wdown_ref[c * FF_CHUNK:(c + 1) * FF_CHUNK, :])
    y_ref[...] = _rms(h, gfin_ref[...])


def _merge_ffn(x, o, u, gates, hist_args, w, *, tile, n_seq, prompt):
    G, R, _ = x.shape
    nt = R // tile
    row = lambda width: pl.BlockSpec((None, tile, width), lambda g, t: (g, t, 0))
    if prompt:
        (u_meta,) = hist_args
        per_tile = tile // HIST_ROWS
        hist_specs = [
            pl.BlockSpec((None, HIST_ROWS, D_POOL), lambda g, t: (g, jnp.maximum(t * per_tile - 1, 0), 0)),
            pl.BlockSpec((HIST_ROWS, D_POOL), lambda g, t: (0, 0)),
        ]
        hist_in = [u, u_meta]
    else:
        (hist,) = hist_args
        hist_specs = [pl.BlockSpec((n_seq, HIST_ROWS, D_POOL), lambda g, t: (t, 0, 0))]
        hist_in = [hist]
    weights = (w["w_attn_br"], w["w_pool_grp"], w["pool_scale"], w["w_pool_br"], w["w_out"],
               w["g_ffn"], w["w_up"], w["w_down"], w["g_final"])
    in_specs = ([row(D_MODEL), row(ATTN_WIDTH), row(D_POOL)] + hist_specs + [row(2 * D_MODEL)]
                + [_const_spec(a.shape) for a in weights])
    return pl.pallas_call(
        functools.partial(_merge_body, n_seq=n_seq, prompt=prompt),
        grid=(G, nt),
        in_specs=in_specs,
        out_specs=row(D_MODEL),
        out_shape=jax.ShapeDtypeStruct((G, R, D_MODEL), F32),
        compiler_params=pltpu.CompilerParams(dimension_semantics=("parallel", "parallel"),
                                             vmem_limit_bytes=VMEM_LIMIT),
        name="merge_ffn",
    )(x, o, u, *hist_in, gates, *weights)


def _rope_tables(pos):
    half = QK_ROPE // 2
    inv_freq = jnp.exp(-math.log(ROPE_BASE) * jnp.arange(half, dtype=jnp.float32) / half)
    ang = pos.astype(jnp.float32)[:, None] * inv_freq[None, :]
    c, s = jnp.cos(ang), jnp.sin(ang)
    reps = LANES // QK_ROPE
    return jnp.concatenate([c, c] * reps, axis=-1), jnp.concatenate([-s, s] * reps, axis=-1)


def _prep_weights(l, w_in, g_norm_mix, g_q, g_kv, w_q_up, w_uk, w_uv, w_attn_br, w_pool_grp,
                  pool_scale, w_pool_br, w_out, g_norm_ffn, w_up, w_down, g_final):
    half = QK_ROPE // 2
    i0, i1, i2, i3 = Q_LORA, Q_LORA + KV_LORA, Q_LORA + KV_LORA + QK_ROPE, Q_LORA + KV_LORA + QK_ROPE + D_POOL
    wi = w_in[l]
    w_kr = wi[:, i1:i2]
    w_kr_sw = jnp.concatenate([w_kr[:, half:], w_kr[:, :half]], axis=-1)
    wq = w_q_up[l].reshape(Q_LORA, N_HEADS, QK_NOPE + QK_ROPE)
    wq_r1, wq_r2 = wq[:, :, QK_NOPE:QK_NOPE + half], wq[:, :, QK_NOPE + half:]
    row = lambda v: v.reshape(1, -1).astype(F32)
    return {
        "g_mix": row(g_norm_mix[l]), "g_q": row(g_q[l]), "g_kv": row(g_kv[l]),
        "w_qlat": wi[:, :i0].astype(BF16),
        "w_kvr": jnp.concatenate([wi[:, i0:i1], w_kr, w_kr, w_kr_sw, w_kr_sw], axis=-1).astype(BF16),
        "w_u": wi[:, i2:i3].astype(BF16),
        "w_g": wi[:, i3:].astype(BF16),
        "w_qup": jnp.concatenate([
            wq[:, :, :QK_NOPE].reshape(Q_LORA, -1),
            jnp.concatenate([wq_r1, wq_r2], axis=-1).reshape(Q_LORA, -1),
            jnp.concatenate([wq_r2, wq_r1], axis=-1).reshape(Q_LORA, -1)], axis=-1).astype(BF16),
        "w_uk": w_uk[l].reshape(KV_LORA, N_HEADS * QK_NOPE).astype(BF16),
        "w_uvt": w_uv[l].reshape(KV_LORA, N_HEADS * V_HEAD).T.astype(BF16),
        "w_ukt": jnp.transpose(w_uk[l], (1, 2, 0)).astype(BF16),
        "w_uv3": jnp.transpose(w_uv[l], (1, 0, 2)).astype(BF16),
        "w_attn_br": w_attn_br[l].astype(BF16),
        "w_pool_grp": w_pool_grp[l].astype(BF16),
        "pool_scale": row(pool_scale[l]),
        "w_pool_br": w_pool_br[l].astype(BF16),
        "w_out": w_out[l].astype(BF16),
        "g_ffn": row(g_norm_ffn[l]),
        "w_up": w_up[l].astype(BF16),
        "w_down": w_down[l].astype(BF16),
        "g_final": row(g_final),
    }


def kernel(x_prompt, x_sample, cache_kv_latent, cache_k_rope, cache_pool, meta_tokens, w_in, g_norm_mix, g_q, g_kv, w_q_up, w_uk, w_uv, w_attn_br, w_pool_grp, pool_scale, w_pool_br, w_out, g_norm_ffn, w_up, w_down, g_final):
    B, S, _ = x_prompt.shape
    Bd, T, _ = x_sample.shape
    past = cache_kv_latent.shape[2]
    assert w_in.shape[0] == 1 and S % ROW_TILE == 0 and (Bd * T) % ROW_TILE == 0 and ROW_TILE % T == 0
    assert N_META >= max(POOL_WINDOWS) and POOL_HIST + 1 >= max(POOL_WINDOWS) and T >= POOL_HIST
    w = _prep_weights(0, w_in, g_norm_mix, g_q, g_kv, w_q_up, w_uk, w_uv, w_attn_br, w_pool_grp,
                      pool_scale, w_pool_br, w_out, g_norm_ffn, w_up, w_down, g_final)

    meta = jnp.concatenate([meta_tokens.astype(F32), jnp.zeros((META_PAD - N_META, D_MODEL), F32)], axis=0)
    cos_m, sin_m = _rope_tables(jnp.arange(META_PAD, dtype=jnp.int32))
    pm = _project(meta[None], cos_m, sin_m, w, tile=META_PAD, emit_q=False, emit_kv=True,
                  emit_gates=False, emit_kpe2=False)

    cos_p, sin_p = _rope_tables(N_META + jnp.arange(S, dtype=jnp.int32))
    pp = _project(x_prompt, cos_p, sin_p, w, tile=ROW_TILE, emit_q=True, emit_kv=True,
                  emit_gates=True, emit_kpe2=False)
    o_p = _flash(pp["q"], pp["k"], pp["vt"], pm["k"], pm["vt"])
    y_prompt = _merge_ffn(x_prompt, o_p, pp["u"], pp["gates"], (pm["u"][0],), w,
                          tile=ROW_TILE, n_seq=1, prompt=True)

    rs = Bd * T
    cos_s, sin_s = _rope_tables(jnp.tile(past + jnp.arange(T, dtype=jnp.int32), Bd))
    xs = x_sample.reshape(1, rs, D_MODEL)
    ps = _project(xs, cos_s, sin_s, w, tile=ROW_TILE, emit_q=True, emit_kv=False,
                  emit_gates=True, emit_kpe2=True)
    cache_pe2 = jnp.concatenate([cache_k_rope[0]] * (LANES // QK_ROPE), axis=-1).astype(BF16)
    o_s = _decode(ps["q"][0], ps["c"][0], ps["kpe2"][0], cache_kv_latent[0], cache_pe2,
                  w["w_ukt"], w["w_uv3"], seq=T)
    hist_s = jnp.concatenate([jnp.zeros((Bd, HIST_ROWS - POOL_HIST, D_POOL), F32), cache_pool[0]], axis=1)
    y_sample = _merge_ffn(xs, o_s[None], ps["u"], ps["gates"], (hist_s,), w,
                          tile=ROW_TILE, n_seq=ROW_TILE // T, prompt=False).reshape(Bd, T, D_MODEL)

    def with_meta(m, f):
        return jnp.concatenate([jnp.broadcast_to(m[:, :N_META], (B,) + m[0, :N_META].shape), f], axis=1)[None]

    c_s = ps["c"].reshape(Bd, T, KV_LORA)
    pe_s = ps["kpe"].reshape(Bd, T, QK_ROPE)
    u_s = ps["u"].reshape(Bd, T, D_POOL)
    return (y_prompt, y_sample,
            with_meta(pm["c"], pp["c"]), with_meta(pm["kpe"], pp["kpe"]), pp["u"][:, -POOL_HIST:][None],
            c_s[None], pe_s[None], u_s[:, -POOL_HIST:][None])
```

```python
import functools
import math

import jax
import jax.numpy as jnp
from jax import lax
from jax.experimental import pallas as pl
from jax.experimental.pallas import tpu as pltpu

F32 = jnp.float32
BF16 = jnp.bfloat16

D_MODEL = 1024
N_HEADS = 8
Q_LORA = 384
KV_LORA = 256
QK_NOPE = 128
QK_ROPE = 64
V_HEAD = 128
ATTN_WIDTH = N_HEADS * V_HEAD
POOL_WINDOWS = (2, 4, 8, 16)
POOL_GROUP = 128
D_POOL = len(POOL_WINDOWS) * POOL_GROUP
POOL_HIST = max(POOL_WINDOWS) - 1
HIST_ROWS = POOL_HIST + 1
D_FF = 4 * D_MODEL
N_META = 16
CHUNK = 64
ROPE_BASE = 10000.0
EPS = 1e-6
SM_SCALE = (QK_NOPE + QK_ROPE) ** -0.5
LOG2E = 1.4426950408889634
NEG_INF = -1e30

LANES = 128
HEAD_W = QK_NOPE + LANES
V_ROWS = V_HEAD + 16
ROW_TILE = 512
META_PAD = 128
Q_STREAMS = 4
FF_CHUNK = 1024
VMEM_LIMIT = 56 * 1024 * 1024


def _nt(a, b):
    return lax.dot_general(a, b, (((1,), (1,)), ((), ())), preferred_element_type=F32)


def _mm(a, b):
    return jnp.dot(a, b, preferred_element_type=F32)


def _rms(x, g):
    return x * lax.rsqrt(jnp.mean(x * x, axis=-1, keepdims=True) + EPS) * g


def _const_spec(shape):
    nd = len(shape)
    return pl.BlockSpec(shape, lambda *_: (0,) * nd, pipeline_mode=pl.Buffered(1))


def _project_body(x_ref, cos_ref, sin_ref, *rest, emit_q, emit_kv, emit_gates, emit_kpe2):
    rest = list(rest)
    cos_t_ref, sin_t_ref = (rest.pop(0), rest.pop(0)) if emit_q == "transposed" else (None, None)
    gmix_ref, gq_ref, gkv_ref, wql_ref, wkvr_ref, wu_ref, wg_ref, wqup_ref, wuk_ref, wuvt_ref = rest[:10]
    outs = rest[10:]
    q_ref = outs.pop(0) if emit_q else None
    k_ref, vt_ref = (outs.pop(0), outs.pop(0)) if emit_kv else (None, None)
    c_ref, kpe_ref, u_ref = outs.pop(0), outs.pop(0), outs.pop(0)
    g_ref = outs.pop(0) if emit_gates else None
    kpe2_ref = outs.pop(0) if emit_kpe2 else None

    rows = x_ref.shape[0]
    xn = _rms(x_ref[...], gmix_ref[...]).astype(BF16)
    cos = cos_ref[...]
    sin = sin_ref[...]
    lane = lax.broadcasted_iota(jnp.int32, (rows, LANES), 1)
    half_masks = (lane < QK_ROPE, lane >= QK_ROPE)

    kvr = _mm(xn, wkvr_ref[...])
    c = _rms(kvr[:, :KV_LORA], gkv_ref[...])
    c_ref[...] = c
    kpe2 = kvr[:, KV_LORA:KV_LORA + LANES] * cos + kvr[:, KV_LORA + LANES:] * sin
    kpe_ref[...] = kpe2[:, :QK_ROPE]
    if emit_kpe2:
        kpe2_ref[...] = kpe2.astype(BF16)
    cb = c.astype(BF16)

    if emit_kv:
        k_nope = _mm(cb, wuk_ref[...])
        for h in range(N_HEADS):
            k_ref[:, h * HEAD_W:h * HEAD_W + QK_NOPE] = k_nope[:, h * QK_NOPE:(h + 1) * QK_NOPE].astype(BF16)
            k_ref[:, h * HEAD_W + QK_NOPE:(h + 1) * HEAD_W] = jnp.where(half_masks[h % 2], kpe2, 0.0).astype(BF16)
        vt = _nt(wuvt_ref[...], cb)
        extra = V_ROWS - V_HEAD
        ones_rows = jnp.where(lax.broadcasted_iota(jnp.int32, (extra, rows), 0) == 0, 1.0, 0.0).astype(BF16)
        for h in range(N_HEADS):
            vt_ref[h * V_ROWS:h * V_ROWS + V_HEAD, :] = vt[h * V_HEAD:(h + 1) * V_HEAD].astype(BF16)
            vt_ref[h * V_ROWS + V_HEAD:(h + 1) * V_ROWS, :] = ones_rows

    u_ref[...] = _mm(xn, wu_ref[...])

    if emit_gates:
        logits = _mm(xn, wg_ref[...])
        g_ref[...] = (1.0 / (1.0 + jnp.exp(-logits))).astype(BF16)

    nope_w = N_HEADS * QK_NOPE
    rope_w = N_HEADS * QK_ROPE
    if emit_q == "transposed":
        qn = _rms(_mm(xn, wql_ref[...]), gq_ref[...]).astype(BF16)
        qall = _nt(wqup_ref[...], qn) * (SM_SCALE * LOG2E)
        cos_t = cos_t_ref[...]
        sin_t = sin_t_ref[...]
        sub = lax.broadcasted_iota(jnp.int32, (LANES, rows), 0)
        sub_masks = (sub < QK_ROPE, sub >= QK_ROPE)
        for h in range(N_HEADS):
            j = h // 2
            r = qall[nope_w + j * LANES:nope_w + (j + 1) * LANES] * cos_t
            r = r + qall[nope_w + rope_w + j * LANES:nope_w + rope_w + (j + 1) * LANES] * sin_t
            q_ref[h * HEAD_W:h * HEAD_W + QK_NOPE, :] = qall[h * QK_NOPE:(h + 1) * QK_NOPE].astype(BF16)
            q_ref[h * HEAD_W + QK_NOPE:(h + 1) * HEAD_W, :] = jnp.where(sub_masks[h % 2], r, 0.0).astype(BF16)
    elif emit_q:
        qn = _rms(_mm(xn, wql_ref[...]), gq_ref[...]).astype(BF16)
        qall = _mm(qn, wqup_ref[...]) * (SM_SCALE * LOG2E)
        for h in range(N_HEADS):
            j = h // 2
            r = qall[:, nope_w + j * LANES:nope_w + (j + 1) * LANES] * cos
            r = r + qall[:, nope_w + rope_w + j * LANES:nope_w + rope_w + (j + 1) * LANES] * sin
            q_ref[:, h * HEAD_W:h * HEAD_W + QK_NOPE] = qall[:, h * QK_NOPE:(h + 1) * QK_NOPE].astype(BF16)
            q_ref[:, h * HEAD_W + QK_NOPE:(h + 1) * HEAD_W] = jnp.where(half_masks[h % 2], r, 0.0).astype(BF16)


def _project(x, cos, sin, w, *, tile, emit_q, emit_kv, emit_gates, emit_kpe2):
    G, R, _ = x.shape
    nt = R // tile
    assert nt * tile == R
    row = lambda width: pl.BlockSpec((None, tile, width), lambda g, t: (g, t, 0))
    tab = pl.BlockSpec((tile, LANES), lambda g, t: (t, 0))
    q_t = emit_q == "transposed"
    weights = (w["g_mix"], w["g_q"], w["g_kv"], w["w_qlat"], w["w_kvr"], w["w_u"], w["w_g"],
               w["w_qup_t"] if q_t else w["w_qup"], w["w_uk"], w["w_uvt"])
    tables = [cos, sin]
    in_specs = [row(D_MODEL), tab, tab]
    if q_t:
        tables += [cos.T, sin.T]
        in_specs += [pl.BlockSpec((LANES, tile), lambda g, t: (0, t))] * 2
    in_specs += [_const_spec(a.shape) for a in weights]
    out_shape, out_specs = [], []

    def add(shape, spec, dtype):
        out_shape.append(jax.ShapeDtypeStruct(shape, dtype))
        out_specs.append(spec)

    if q_t:
        add((G, N_HEADS * HEAD_W, R), pl.BlockSpec((None, N_HEADS * HEAD_W, tile), lambda g, t: (g, 0, t)), BF16)
    elif emit_q:
        add((G, R, N_HEADS * HEAD_W), row(N_HEADS * HEAD_W), BF16)
    if emit_kv:
        add((G, nt, tile, N_HEADS * HEAD_W),
            pl.BlockSpec((None, None, tile, N_HEADS * HEAD_W), lambda g, t: (g, t, 0, 0)), BF16)
        add((G, nt, N_HEADS * V_ROWS, tile),
            pl.BlockSpec((None, None, N_HEADS * V_ROWS, tile), lambda g, t: (g, t, 0, 0)), BF16)
    add((G, R, KV_LORA), row(KV_LORA), F32)
    add((G, R, QK_ROPE), row(QK_ROPE), F32)
    add((G, R, D_POOL), row(D_POOL), F32)
    if emit_gates:
        add((G, R, 2 * D_MODEL), row(2 * D_MODEL), BF16)
    if emit_kpe2:
        add((G, R, LANES), row(LANES), BF16)

    body = functools.partial(_project_body, emit_q=emit_q, emit_kv=emit_kv, emit_gates=emit_gates,
                             emit_kpe2=emit_kpe2)
    outs = pl.pallas_call(
        body,
        grid=(G, nt),
        in_specs=in_specs,
        out_specs=out_specs,
        out_shape=out_shape,
        compiler_params=pltpu.CompilerParams(dimension_semantics=("parallel", "parallel"),
                                             vmem_limit_bytes=VMEM_LIMIT),
        name="project",
    )(x, *tables, *weights)
    outs = list(outs)
    res = {}
    if emit_q:
        res["q"] = outs.pop(0)
    if emit_kv:
        res["k"], res["vt"] = outs.pop(0), outs.pop(0)
    res["c"], res["kpe"], res["u"] = outs.pop(0), outs.pop(0), outs.pop(0)
    if emit_gates:
        res["gates"] = outs.pop(0)
    if emit_kpe2:
        res["kpe2"] = outs.pop(0)
    return res


def _flash_body(qt_ref, k_ref, vt_ref, km_ref, vtm_ref, o_ref, m_ref, acc_ref, s_ref, cm_ref, *, streams):
    tk = k_ref.shape[1]
    base = pl.program_id(2) * streams

    def produce(buf, kb, ss):
        for s in ss:
            s_t = _mm(k_ref[kb], qt_ref[:, s * tk:(s + 1) * tk])
            s_ref[buf, s] = s_t
            cm_ref[buf, s] = jnp.max(s_t, axis=0, keepdims=True)

    def own_block(buf, s):
        n = tk // LANES
        diag = (lax.broadcasted_iota(jnp.int32, (LANES, LANES), 0) // CHUNK
                <= lax.broadcasted_iota(jnp.int32, (LANES, LANES), 1) // CHUNK)
        hidden = jnp.full((LANES, LANES), NEG_INF, F32)
        rows = []
        for i in range(n):
            cols = []
            for j in range(n):
                if i > j:
                    cols.append(hidden)
                    continue
                t = s_ref[buf, s, i * LANES:(i + 1) * LANES, j * LANES:(j + 1) * LANES]
                cols.append(t if i < j else jnp.where(diag, t, NEG_INF))
            rows.append(jnp.concatenate(cols, axis=1))
        return jnp.concatenate(rows, axis=0)

    def consume(buf, s, kb, own=False):
        if own:
            s_t = own_block(buf, s)
            m_blk = jnp.max(s_t, axis=0, keepdims=True)
        else:
            s_t = s_ref[buf, s]
            m_blk = cm_ref[buf, s]
        m_old = m_ref[s]
        m_new = jnp.maximum(m_old, m_blk)
        p = jnp.exp2(s_t - m_new)
        acc_ref[s] = jnp.exp2(m_old - m_new) * acc_ref[s] + _mm(vt_ref[kb], p.astype(BF16))
        m_ref[s] = m_new

    everyone = range(streams)

    s_m = _mm(km_ref[...], qt_ref[...])
    produce(0, 0, everyone)
    s_m = jnp.where(lax.broadcasted_iota(jnp.int32, s_m.shape, 0) < N_META, s_m, NEG_INF)
    m_0 = jnp.max(s_m, axis=0, keepdims=True)
    acc_0 = _mm(vtm_ref[...], jnp.exp2(s_m - m_0).astype(BF16))
    for s in everyone:
        m_ref[s] = m_0[:, s * tk:(s + 1) * tk]
        acc_ref[s] = acc_0[:, s * tk:(s + 1) * tk]

    def step(rd, wr, kb):
        lead = 1
        for i in range(streams + lead):
            if i < streams:
                produce(wr, kb + 1, [i])
            if i >= lead:
                consume(rd, i - lead, kb)

    per_trip = 4 if streams % 4 == 0 else 2

    def trip(j, carry):
        for i in range(per_trip):
            step(i % 2, 1 - i % 2, per_trip * j + i)
        return carry

    lax.fori_loop(0, base // per_trip, trip, 0)

    for w in everyone:
        buf = w % 2
        if w + 1 < streams:
            produce(1 - buf, base + w + 1, range(w + 1, streams))
        for s in range(w, streams):
            consume(buf, s, base + w, own=(s == w))

    for s in everyone:
        acc = acc_ref[s]
        o_ref[s * tk:(s + 1) * tk, :] = (acc[:V_HEAD] / acc[V_HEAD:V_HEAD + 1]).T.astype(BF16)


def _flash(q, k, vt, k_meta, vt_meta):
    B, _, S = q.shape
    nt, tile = k.shape[1], k.shape[2]
    streams = Q_STREAMS
    assert nt % streams == 0 and streams % 2 == 0
    return pl.pallas_call(
        functools.partial(_flash_body, streams=streams),
        grid=(B, N_HEADS, nt // streams),
        in_specs=[
            pl.BlockSpec((None, HEAD_W, streams * tile), lambda b, h, i: (b, h, i)),
            pl.BlockSpec((None, nt, tile, HEAD_W), lambda b, h, i: (b, 0, 0, h)),
            pl.BlockSpec((None, nt, V_ROWS, tile), lambda b, h, i: (b, 0, h, 0)),
            pl.BlockSpec((None, None, META_PAD, HEAD_W), lambda b, h, i: (0, 0, 0, h)),
            pl.BlockSpec((None, None, V_ROWS, META_PAD), lambda b, h, i: (0, 0, h, 0)),
        ],
        out_specs=pl.BlockSpec((None, streams * tile, V_HEAD), lambda b, h, i: (b, i, h)),
        out_shape=jax.ShapeDtypeStruct((B, S, ATTN_WIDTH), BF16),
        scratch_shapes=[pltpu.VMEM((streams, 1, tile), F32),
                        pltpu.VMEM((streams, V_ROWS, tile), F32),
                        pltpu.VMEM((2, streams, tile, tile), F32), pltpu.VMEM((2, streams, 1, tile), F32)],
        compiler_params=pltpu.CompilerParams(dimension_semantics=("parallel", "parallel", "arbitrary"),
                                             vmem_limit_bytes=VMEM_LIMIT),
        name="flash",
    )(q, k, vt, k_meta, vt_meta)


def _decode_body(q_ref, cn_ref, pn_ref, cc_ref, pc_ref, wukt_ref, wuv_ref, o_ref):
    q = q_ref[...]
    t = q.shape[0]
    qa = jnp.concatenate(
        [_mm(q[:, h * HEAD_W:h * HEAD_W + QK_NOPE], wukt_ref[h]) for h in range(N_HEADS)], axis=0).astype(BF16)
    qp = jnp.concatenate([q[:, h * HEAD_W + QK_NOPE:(h + 1) * HEAD_W] for h in range(N_HEADS)], axis=0)
    cc = cc_ref[...].astype(BF16)
    cn = cn_ref[...].astype(BF16)
    s_c = _nt(qa, cc) + _nt(qp, pc_ref[...])
    s_n = _nt(qa, cn) + _nt(qp, pn_ref[...])
    m = jnp.maximum(jnp.max(s_c, axis=-1, keepdims=True), jnp.max(s_n, axis=-1, keepdims=True))
    p_c = jnp.exp2(s_c - m)
    p_n = jnp.exp2(s_n - m)
    l = jnp.sum(p_c, axis=-1, keepdims=True) + jnp.sum(p_n, axis=-1, keepdims=True)
    o_lat = ((_mm(p_c.astype(BF16), cc) + _mm(p_n.astype(BF16), cn)) / l).astype(BF16)
    for h in range(N_HEADS):
        o_ref[:, h * V_HEAD:(h + 1) * V_HEAD] = _mm(o_lat[h * t:(h + 1) * t], wuv_ref[h]).astype(BF16)


def _decode(q, c_new, kpe2_new, cache_c, cache_pe2, wukt, wuv, *, seq):
    R = q.shape[0]
    nb, past, _ = cache_c.shape
    assert nb * seq == R
    return pl.pallas_call(
        _decode_body,
        grid=(nb,),
        in_specs=[
            pl.BlockSpec((seq, N_HEADS * HEAD_W), lambda i: (i, 0)),
            pl.BlockSpec((seq, KV_LORA), lambda i: (i, 0)),
            pl.BlockSpec((seq, LANES), lambda i: (i, 0)),
            pl.BlockSpec((None, past, KV_LORA), lambda i: (i, 0, 0)),
            pl.BlockSpec((None, past, LANES), lambda i: (i, 0, 0)),
            _const_spec(wukt.shape),
            _const_spec(wuv.shape),
        ],
        out_specs=pl.BlockSpec((seq, ATTN_WIDTH), lambda i: (i, 0)),
        out_shape=jax.ShapeDtypeStruct((R, ATTN_WIDTH), BF16),
        compiler_params=pltpu.CompilerParams(dimension_semantics=("parallel",),
                                             vmem_limit_bytes=VMEM_LIMIT),
        name="decode",
    )(q, c_new, kpe2_new, cache_c, cache_pe2, wukt, wuv)


def _pool_seq(hist, u):
    ext = jnp.concatenate([hist, u], axis=0)
    outs = []
    for g, w in enumerate(POOL_WINDOWS):
        e = ext[:, g * POOL_GROUP:(g + 1) * POOL_GROUP]
        s = e
        shift = 1
        while shift < w:
            s = s + pltpu.roll(s, shift, axis=0)
            shift *= 2
        outs.append(s[HIST_ROWS:] * (1.0 / w) - e[HIST_ROWS:])
    return outs


def _merge_body(*refs, n_seq, prompt):
    if prompt:
        (x_ref, o_ref, u_ref, hprev_ref, hmeta_ref, g_ref, wab_ref, wpg_ref, ps_ref, wpb_ref, wout_ref,
         gffn_ref, wup_ref, wdown_ref, gfin_ref, y_ref) = refs
        first = pl.program_id(1) == 0
        hists = [jnp.where(first, hmeta_ref[...], hprev_ref[...])]
    else:
        (x_ref, o_ref, u_ref, hist_ref, g_ref, wab_ref, wpg_ref, ps_ref, wpb_ref, wout_ref,
         gffn_ref, wup_ref, wdown_ref, gfin_ref, y_ref) = refs
        hists = [hist_ref[i] for i in range(n_seq)]

    rows = x_ref.shape[0]
    t = rows // n_seq
    u = u_ref[...]
    per_seq = [_pool_seq(hists[i], u[i * t:(i + 1) * t]) for i in range(n_seq)]
    mixed = []
    for g in range(len(POOL_WINDOWS)):
        pooled = jnp.concatenate([per_seq[i][g] for i in range(n_seq)], axis=0) if n_seq > 1 else per_seq[0][g]
        mixed.append(_mm(pooled.astype(BF16), wpg_ref[g]))
    pm = (jnp.concatenate(mixed, axis=-1) * ps_ref[...]).astype(BF16)
    p = _mm(pm, wpb_ref[...])
    a = _mm(o_ref[...], wab_ref[...])
    gates = g_ref[...].astype(F32)
    mix = (gates[:, :D_MODEL] * a + gates[:, D_MODEL:] * p).astype(BF16)
    h = x_ref[...] + _mm(mix, wout_ref[...])
    hn = _rms(h, gffn_ref[...]).astype(BF16)
    for c in range(D_FF // FF_CHUNK):
        up = jnp.maximum(_mm(hn, wup_ref[:, c * FF_CHUNK:(c + 1) * FF_CHUNK]), 0.0)
        h = h + _mm((up * up).astype(BF16), wdown_ref[c * FF_CHUNK:(c + 1) * FF_CHUNK, :])
    y_ref[...] = _rms(h, gfin_ref[...])


def _merge_ffn(x, o, u, gates, hist_args, w, *, tile, n_seq, prompt):
    G, R, _ = x.shape
    nt = R // tile
    row = lambda width: pl.BlockSpec((None, tile, width), lambda g, t: (g, t, 0))
    if prompt:
        (u_meta,) = hist_args
        per_tile = tile // HIST_ROWS
        hist_specs = [
            pl.BlockSpec((None, HIST_ROWS, D_POOL), lambda g, t: (g, jnp.maximum(t * per_tile - 1, 0), 0)),
            pl.BlockSpec((HIST_ROWS, D_POOL), lambda g, t: (0, 0)),
        ]
        hist_in = [u, u_meta]
    else:
        (hist,) = hist_args
        hist_specs = [pl.BlockSpec((n_seq, HIST_ROWS, D_POOL), lambda g, t: (t, 0, 0))]
        hist_in = [hist]
    weights = (w["w_attn_br"], w["w_pool_grp"], w["pool_scale"], w["w_pool_br"], w["w_out"],
               w["g_ffn"], w["w_up"], w["w_down"], w["g_final"])
    in_specs = ([row(D_MODEL), row(ATTN_WIDTH), row(D_POOL)] + hist_specs + [row(2 * D_MODEL)]
                + [_const_spec(a.shape) for a in weights])
    return pl.pallas_call(
        functools.partial(_merge_body, n_seq=n_seq, prompt=prompt),
        grid=(G, nt),
        in_specs=in_specs,
        out_specs=row(D_MODEL),
        out_shape=jax.ShapeDtypeStruct((G, R, D_MODEL), F32),
        compiler_params=pltpu.CompilerParams(dimension_semantics=("parallel", "parallel"),
                                             vmem_limit_bytes=VMEM_LIMIT),
        name="merge_ffn",
    )(x, o, u, *hist_in, gates, *weights)


def _rope_tables(pos):
    half = QK_ROPE // 2
    inv_freq = jnp.exp(-math.log(ROPE_BASE) * jnp.arange(half, dtype=jnp.float32) / half)
    ang = pos.astype(jnp.float32)[:, None] * inv_freq[None, :]
    c, s = jnp.cos(ang), jnp.sin(ang)
    reps = LANES // QK_ROPE
    return jnp.concatenate([c, c] * reps, axis=-1), jnp.concatenate([-s, s] * reps, axis=-1)


def _prep_weights(l, w_in, g_norm_mix, g_q, g_kv, w_q_up, w_uk, w_uv, w_attn_br, w_pool_grp,
                  pool_scale, w_pool_br, w_out, g_norm_ffn, w_up, w_down, g_final):
    half = QK_ROPE // 2
    i0, i1, i2, i3 = Q_LORA, Q_LORA + KV_LORA, Q_LORA + KV_LORA + QK_ROPE, Q_LORA + KV_LORA + QK_ROPE + D_POOL
    wi = w_in[l]
    w_kr = wi[:, i1:i2]
    w_kr_sw = jnp.concatenate([w_kr[:, half:], w_kr[:, :half]], axis=-1)
    wq = w_q_up[l].reshape(Q_LORA, N_HEADS, QK_NOPE + QK_ROPE)
    wq_r1, wq_r2 = wq[:, :, QK_NOPE:QK_NOPE + half], wq[:, :, QK_NOPE + half:]
    row = lambda v: v.reshape(1, -1).astype(F32)
    w_qup = jnp.concatenate([
        wq[:, :, :QK_NOPE].reshape(Q_LORA, -1),
        jnp.concatenate([wq_r1, wq_r2], axis=-1).reshape(Q_LORA, -1),
        jnp.concatenate([wq_r2, wq_r1], axis=-1).reshape(Q_LORA, -1)], axis=-1).astype(BF16)
    return {
        "g_mix": row(g_norm_mix[l]), "g_q": row(g_q[l]), "g_kv": row(g_kv[l]),
        "w_qlat": wi[:, :i0].astype(BF16),
        "w_kvr": jnp.concatenate([wi[:, i0:i1], w_kr, w_kr, w_kr_sw, w_kr_sw], axis=-1).astype(BF16),
        "w_u": wi[:, i2:i3].astype(BF16),
        "w_g": wi[:, i3:].astype(BF16),
        "w_qup": w_qup,
        "w_qup_t": w_qup.T,
        "w_uk": w_uk[l].reshape(KV_LORA, N_HEADS * QK_NOPE).astype(BF16),
        "w_uvt": w_uv[l].reshape(KV_LORA, N_HEADS * V_HEAD).T.astype(BF16),
        "w_ukt": jnp.transpose(w_uk[l], (1, 2, 0)).astype(BF16),
        "w_uv3": jnp.transpose(w_uv[l], (1, 0, 2)).astype(BF16),
        "w_attn_br": w_attn_br[l].astype(BF16),
        "w_pool_grp": w_pool_grp[l].astype(BF16),
        "pool_scale": row(pool_scale[l]),
        "w_pool_br": w_pool_br[l].astype(BF16),
        "w_out": w_out[l].astype(BF16),
        "g_ffn": row(g_norm_ffn[l]),
        "w_up": w_up[l].astype(BF16),
        "w_down": w_down[l].astype(BF16),
        "g_final": row(g_final),
    }


def kernel(x_prompt, x_sample, cache_kv_latent, cache_k_rope, cache_pool, meta_tokens, w_in, g_norm_mix, g_q, g_kv, w_q_up, w_uk, w_uv, w_attn_br, w_pool_grp, pool_scale, w_pool_br, w_out, g_norm_ffn, w_up, w_down, g_final):
    B, S, _ = x_prompt.shape
    Bd, T, _ = x_sample.shape
    past = cache_kv_latent.shape[2]
    assert w_in.shape[0] == 1 and S % ROW_TILE == 0 and (Bd * T) % ROW_TILE == 0 and ROW_TILE % T == 0
    assert N_META >= max(POOL_WINDOWS) and POOL_HIST + 1 >= max(POOL_WINDOWS) and T >= POOL_HIST
    w = _prep_weights(0, w_in, g_norm_mix, g_q, g_kv, w_q_up, w_uk, w_uv, w_attn_br, w_pool_grp,
                      pool_scale, w_pool_br, w_out, g_norm_ffn, w_up, w_down, g_final)

    meta = jnp.concatenate([meta_tokens.astype(F32), jnp.zeros((META_PAD - N_META, D_MODEL), F32)], axis=0)
    cos_m, sin_m = _rope_tables(jnp.arange(META_PAD, dtype=jnp.int32))
    pm = _project(meta[None], cos_m, sin_m, w, tile=META_PAD, emit_q=False, emit_kv=True,
                  emit_gates=False, emit_kpe2=False)

    cos_p, sin_p = _rope_tables(N_META + jnp.arange(S, dtype=jnp.int32))
    pp = _project(x_prompt, cos_p, sin_p, w, tile=ROW_TILE, emit_q="transposed", emit_kv=True,
                  emit_gates=True, emit_kpe2=False)
    o_p = _flash(pp["q"], pp["k"], pp["vt"], pm["k"], pm["vt"])
    y_prompt = _merge_ffn(x_prompt, o_p, pp["u"], pp["gates"], (pm["u"][0],), w,
                          tile=ROW_TILE, n_seq=1, prompt=True)

    rs = Bd * T
    cos_s, sin_s = _rope_tables(jnp.tile(past + jnp.arange(T, dtype=jnp.int32), Bd))
    xs = x_sample.reshape(1, rs, D_MODEL)
    ps = _project(xs, cos_s, sin_s, w, tile=ROW_TILE, emit_q=True, emit_kv=False,
                  emit_gates=True, emit_kpe2=True)
    cache_pe2 = jnp.concatenate([cache_k_rope[0]] * (LANES // QK_ROPE), axis=-1).astype(BF16)
    o_s = _decode(ps["q"][0], ps["c"][0], ps["kpe2"][0], cache_kv_latent[0], cache_pe2,
                  w["w_ukt"], w["w_uv3"], seq=T)
    hist_s = jnp.concatenate([jnp.zeros((Bd, HIST_ROWS - POOL_HIST, D_POOL), F32), cache_pool[0]], axis=1)
    y_sample = _merge_ffn(xs, o_s[None], ps["u"], ps["gates"], (hist_s,), w,
                          tile=ROW_TILE, n_seq=ROW_TILE // T, prompt=False).reshape(Bd, T, D_MODEL)

    def with_meta(m, f):
        return jnp.concatenate([jnp.broadcast_to(m[:, :N_META], (B,) + m[0, :N_META].shape), f], axis=1)[None]

    c_s = ps["c"].reshape(Bd, T, KV_LORA)
    pe_s = ps["kpe"].reshape(Bd, T, QK_ROPE)
    u_s = ps["u"].reshape(Bd, T, D_POOL)
    return (y_prompt, y_sample,
            with_meta(pm["c"], pp["c"]), with_meta(pm["kpe"], pp["kpe"]), pp["u"][:, -POOL_HIST:][None],
            c_s[None], pe_s[None], u_s[:, -POOL_HIST:][None])
```

```python
import functools
import math

import jax
import jax.numpy as jnp
from jax import lax
from jax.experimental import pallas as pl
from jax.experimental.pallas import tpu as pltpu

F32 = jnp.float32
BF16 = jnp.bfloat16

D_MODEL = 1024
N_HEADS = 8
Q_LORA = 384
KV_LORA = 256
QK_NOPE = 128
QK_ROPE = 64
V_HEAD = 128
ATTN_WIDTH = N_HEADS * V_HEAD
POOL_WINDOWS = (2, 4, 8, 16)
POOL_GROUP = 128
D_POOL = len(POOL_WINDOWS) * POOL_GROUP
POOL_HIST = max(POOL_WINDOWS) - 1
HIST_ROWS = POOL_HIST + 1
D_FF = 4 * D_MODEL
N_META = 16
CHUNK = 64
ROPE_BASE = 10000.0
EPS = 1e-6
SM_SCALE = (QK_NOPE + QK_ROPE) ** -0.5
LOG2E = 1.4426950408889634
NEG_INF = -1e30

LANES = 128
HEAD_W = QK_NOPE + LANES
V_ROWS = V_HEAD + 16
ROW_TILE = 512
META_PAD = 128
Q_STREAMS = 4
FF_CHUNK = 1024
VMEM_LIMIT = 56 * 1024 * 1024


def _nt(a, b):
    return lax.dot_general(a, b, (((1,), (1,)), ((), ())), preferred_element_type=F32)


def _mm(a, b):
    return jnp.dot(a, b, preferred_element_type=F32)


def _rms(x, g):
    return x * lax.rsqrt(jnp.mean(x * x, axis=-1, keepdims=True) + EPS) * g


def _const_spec(shape):
    nd = len(shape)
    return pl.BlockSpec(shape, lambda *_: (0,) * nd, pipeline_mode=pl.Buffered(1))


def _project_body(x_ref, cos_ref, sin_ref, *rest, emit_q, emit_kv, emit_gates, emit_kpe2):
    rest = list(rest)
    cos_t_ref, sin_t_ref = (rest.pop(0), rest.pop(0)) if emit_q == "transposed" else (None, None)
    gmix_ref, gq_ref, gkv_ref, wql_ref, wkvr_ref, wu_ref, wg_ref, wqup_ref, wuk_ref, wuvt_ref = rest[:10]
    outs = rest[10:]
    q_ref = outs.pop(0) if emit_q else None
    k_ref, vt_ref = (outs.pop(0), outs.pop(0)) if emit_kv else (None, None)
    c_ref, kpe_ref, u_ref = outs.pop(0), outs.pop(0), outs.pop(0)
    if len(c_ref.shape) == 3:
        c_ref, kpe_ref = c_ref.at[0], kpe_ref.at[0]
    g_ref = outs.pop(0) if emit_gates else None
    kpe2_ref = outs.pop(0) if emit_kpe2 else None

    rows = x_ref.shape[0]
    xn = _rms(x_ref[...], gmix_ref[...]).astype(BF16)
    cos = cos_ref[...]
    sin = sin_ref[...]
    lane = lax.broadcasted_iota(jnp.int32, (rows, LANES), 1)
    half_masks = (lane < QK_ROPE, lane >= QK_ROPE)

    kvr = _mm(xn, wkvr_ref[...])
    c = _rms(kvr[:, :KV_LORA], gkv_ref[...])
    c_ref[...] = c
    kpe2 = kvr[:, KV_LORA:KV_LORA + LANES] * cos + kvr[:, KV_LORA + LANES:] * sin
    kpe_ref[...] = kpe2[:, :QK_ROPE]
    if emit_kpe2:
        kpe2_ref[...] = kpe2.astype(BF16)
    cb = c.astype(BF16)

    if emit_kv:
        k_nope = _mm(cb, wuk_ref[...])
        for h in range(N_HEADS):
            k_ref[:, h * HEAD_W:h * HEAD_W + QK_NOPE] = k_nope[:, h * QK_NOPE:(h + 1) * QK_NOPE].astype(BF16)
            k_ref[:, h * HEAD_W + QK_NOPE:(h + 1) * HEAD_W] = jnp.where(half_masks[h % 2], kpe2, 0.0).astype(BF16)
        vt = _nt(wuvt_ref[...], cb)
        extra = V_ROWS - V_HEAD
        ones_rows = jnp.where(lax.broadcasted_iota(jnp.int32, (extra, rows), 0) == 0, 1.0, 0.0).astype(BF16)
        for h in range(N_HEADS):
            vt_ref[h * V_ROWS:h * V_ROWS + V_HEAD, :] = vt[h * V_HEAD:(h + 1) * V_HEAD].astype(BF16)
            vt_ref[h * V_ROWS + V_HEAD:(h + 1) * V_ROWS, :] = ones_rows

    u_ref[...] = _mm(xn, wu_ref[...])

    if emit_gates:
        logits = _mm(xn, wg_ref[...])
        g_ref[...] = (1.0 / (1.0 + jnp.exp(-logits))).astype(BF16)

    nope_w = N_HEADS * QK_NOPE
    rope_w = N_HEADS * QK_ROPE
    if emit_q == "transposed":
        qn = _rms(_mm(xn, wql_ref[...]), gq_ref[...]).astype(BF16)
        qall = _nt(wqup_ref[...], qn) * (SM_SCALE * LOG2E)
        half = QK_ROPE // 2
        c_t = cos_t_ref[:half, :]
        s_t = sin_t_ref[half:QK_ROPE, :]
        zeros = jnp.zeros((QK_ROPE, rows), BF16)
        for h in range(N_HEADS):
            r1 = qall[nope_w + h * QK_ROPE:nope_w + h * QK_ROPE + half]
            r2 = qall[nope_w + h * QK_ROPE + half:nope_w + (h + 1) * QK_ROPE]
            r = jnp.concatenate([r1 * c_t - r2 * s_t, r1 * s_t + r2 * c_t], axis=0).astype(BF16)
            own = h * HEAD_W + QK_NOPE + (h % 2) * QK_ROPE
            other = h * HEAD_W + QK_NOPE + (1 - h % 2) * QK_ROPE
            q_ref[h * HEAD_W:h * HEAD_W + QK_NOPE, :] = qall[h * QK_NOPE:(h + 1) * QK_NOPE].astype(BF16)
            q_ref[own:own + QK_ROPE, :] = r
            q_ref[other:other + QK_ROPE, :] = zeros
    elif emit_q:
        qn = _rms(_mm(xn, wql_ref[...]), gq_ref[...]).astype(BF16)
        qall = _mm(qn, wqup_ref[...]) * (SM_SCALE * LOG2E)
        for h in range(N_HEADS):
            j = h // 2
            r = qall[:, nope_w + j * LANES:nope_w + (j + 1) * LANES] * cos
            r = r + qall[:, nope_w + rope_w + j * LANES:nope_w + rope_w + (j + 1) * LANES] * sin
            q_ref[:, h * HEAD_W:h * HEAD_W + QK_NOPE] = qall[:, h * QK_NOPE:(h + 1) * QK_NOPE].astype(BF16)
            q_ref[:, h * HEAD_W + QK_NOPE:(h + 1) * HEAD_W] = jnp.where(half_masks[h % 2], r, 0.0).astype(BF16)


def _project(x, cos, sin, w, *, tile, emit_q, emit_kv, emit_gates, emit_kpe2, lead_rows=0):
    G, R, _ = x.shape
    nt = R // tile
    assert nt * tile == R
    row = lambda width: pl.BlockSpec((None, tile, width), lambda g, t: (g, t, 0))
    tab = pl.BlockSpec((tile, LANES), lambda g, t: (t, 0))
    q_t = emit_q == "transposed"
    weights = (w["g_mix"], w["g_q"], w["g_kv"], w["w_qlat"], w["w_kvr"], w["w_u"], w["w_g"],
               w["w_qup_t"] if q_t else w["w_qup"], w["w_uk"], w["w_uvt"])
    tables = [cos, sin]
    in_specs = [row(D_MODEL), tab, tab]
    if q_t:
        tables += [cos.T, sin.T]
        in_specs += [pl.BlockSpec((LANES, tile), lambda g, t: (0, t))] * 2
    in_specs += [_const_spec(a.shape) for a in weights]
    out_shape, out_specs = [], []

    def add(shape, spec, dtype):
        out_shape.append(jax.ShapeDtypeStruct(shape, dtype))
        out_specs.append(spec)

    if q_t:
        add((G, N_HEADS * HEAD_W, R), pl.BlockSpec((None, N_HEADS * HEAD_W, tile), lambda g, t: (g, 0, t)), BF16)
    elif emit_q:
        add((G, R, N_HEADS * HEAD_W), row(N_HEADS * HEAD_W), BF16)
    if emit_kv:
        add((G, nt, tile, N_HEADS * HEAD_W),
            pl.BlockSpec((None, None, tile, N_HEADS * HEAD_W), lambda g, t: (g, t, 0, 0)), BF16)
        add((G, nt, N_HEADS * V_ROWS, tile),
            pl.BlockSpec((None, None, N_HEADS * V_ROWS, tile), lambda g, t: (g, t, 0, 0)), BF16)
    cache = row if lead_rows == 0 else (lambda width: pl.BlockSpec(
        (pl.Element(1), pl.Element(tile), pl.Element(width)),
        lambda g, t: (g, pl.multiple_of(lead_rows + t * tile, math.gcd(lead_rows, tile)), 0)))
    add((G, lead_rows + R, KV_LORA), cache(KV_LORA), F32)
    add((G, lead_rows + R, QK_ROPE), cache(QK_ROPE), F32)
    add((G, R, D_POOL), row(D_POOL), F32)
    if emit_gates:
        add((G, R, 2 * D_MODEL), row(2 * D_MODEL), BF16)
    if emit_kpe2:
        add((G, R, LANES), row(LANES), BF16)

    body = functools.partial(_project_body, emit_q=emit_q, emit_kv=emit_kv, emit_gates=emit_gates,
                             emit_kpe2=emit_kpe2)
    outs = pl.pallas_call(
        body,
        grid=(G, nt),
        in_specs=in_specs,
        out_specs=out_specs,
        out_shape=out_shape,
        compiler_params=pltpu.CompilerParams(dimension_semantics=("parallel", "parallel"),
                                             vmem_limit_bytes=VMEM_LIMIT),
        name="project",
    )(x, *tables, *weights)
    outs = list(outs)
    res = {}
    if emit_q:
        res["q"] = outs.pop(0)
    if emit_kv:
        res["k"], res["vt"] = outs.pop(0), outs.pop(0)
    res["c"], res["kpe"], res["u"] = outs.pop(0), outs.pop(0), outs.pop(0)
    if emit_gates:
        res["gates"] = outs.pop(0)
    if emit_kpe2:
        res["kpe2"] = outs.pop(0)
    return res


def _fill_lead_body(big_ref, rows_ref, out_ref):
    del big_ref
    out_ref[...] = rows_ref[...]


def _fill_lead_rows(big, rows):
    G, _, W = big.shape
    n = rows.shape[0]
    return pl.pallas_call(
        _fill_lead_body,
        grid=(G,),
        in_specs=[pl.BlockSpec(memory_space=pl.ANY), pl.BlockSpec((n, W), lambda g: (0, 0))],
        out_specs=pl.BlockSpec((None, n, W), lambda g: (g, 0, 0)),
        out_shape=jax.ShapeDtypeStruct(big.shape, big.dtype),
        input_output_aliases={0: 0},
        name="fill_lead_rows",
    )(big, rows)


def _flash_body(qt_ref, k_ref, vt_ref, km_ref, vtm_ref, o_ref, m_ref, acc_ref, s_ref, cm_ref, *, streams):
    tk = k_ref.shape[1]
    base = pl.program_id(2) * streams

    def produce(buf, kb, ss):
        for s in ss:
            s_t = _mm(k_ref[kb], qt_ref[:, s * tk:(s + 1) * tk])
            s_ref[buf, s] = s_t
            cm_ref[buf, s] = jnp.max(s_t, axis=0, keepdims=True)

    def own_columns(buf, s):
        diag = (lax.broadcasted_iota(jnp.int32, (LANES, LANES), 0) // CHUNK
                <= lax.broadcasted_iota(jnp.int32, (LANES, LANES), 1) // CHUNK)
        cols = []
        for j in range(tk // LANES):
            tiles = [s_ref[buf, s, i * LANES:(i + 1) * LANES, j * LANES:(j + 1) * LANES] for i in range(j + 1)]
            tiles[j] = jnp.where(diag, tiles[j], NEG_INF)
            cols.append(jnp.concatenate(tiles, axis=0))
        return cols

    def consume(buf, s, kb, own=False):
        m_old = m_ref[s]
        if own:
            cols = own_columns(buf, s)
            m_blk = jnp.concatenate([jnp.max(c, axis=0, keepdims=True) for c in cols], axis=1)
            m_new = jnp.maximum(m_old, m_blk)
            p_cols = []
            for j, c in enumerate(cols):
                p_j = jnp.exp2(c - m_new[:, j * LANES:(j + 1) * LANES]).astype(BF16)
                if c.shape[0] < tk:
                    p_j = jnp.concatenate([p_j, jnp.zeros((tk - c.shape[0], LANES), BF16)], axis=0)
                p_cols.append(p_j)
            p = jnp.concatenate(p_cols, axis=1)
        else:
            m_new = jnp.maximum(m_old, cm_ref[buf, s])
            p = jnp.exp2(s_ref[buf, s] - m_new).astype(BF16)
        acc_ref[s] = jnp.exp2(m_old - m_new) * acc_ref[s] + _mm(vt_ref[kb], p)
        m_ref[s] = m_new

    everyone = range(streams)

    s_m = _mm(km_ref[...], qt_ref[...])
    produce(0, 0, everyone)
    s_m = jnp.where(lax.broadcasted_iota(jnp.int32, s_m.shape, 0) < N_META, s_m, NEG_INF)
    m_0 = jnp.max(s_m, axis=0, keepdims=True)
    acc_0 = _mm(vtm_ref[...], jnp.exp2(s_m - m_0).astype(BF16))
    for s in everyone:
        m_ref[s] = m_0[:, s * tk:(s + 1) * tk]
        acc_ref[s] = acc_0[:, s * tk:(s + 1) * tk]

    def step(rd, wr, kb):
        lead = 1
        for i in range(streams + lead):
            if i < streams:
                produce(wr, kb + 1, [i])
            if i >= lead:
                consume(rd, i - lead, kb)

    per_trip = 4 if streams % 4 == 0 else 2

    def trip(j, carry):
        for i in range(per_trip):
            step(i % 2, 1 - i % 2, per_trip * j + i)
        return carry

    lax.fori_loop(0, base // per_trip, trip, 0)

    for w in everyone:
        buf = w % 2
        later = list(range(w + 1, streams))
        produce(1 - buf, base + w + 1, later[:1])
        for n, s in enumerate(range(w, streams)):
            produce(1 - buf, base + w + 1, later[n + 1:n + 2])
            consume(buf, s, base + w, own=(s == w))

    for s in everyone:
        acc = acc_ref[s]
        o_ref[s * tk:(s + 1) * tk, :] = (acc[:V_HEAD] / acc[V_HEAD:V_HEAD + 1]).T.astype(BF16)


def _flash(q, k, vt, k_meta, vt_meta):
    B, _, S = q.shape
    nt, tile = k.shape[1], k.shape[2]
    streams = Q_STREAMS
    assert nt % streams == 0 and streams % 2 == 0
    return pl.pallas_call(
        functools.partial(_flash_body, streams=streams),
        grid=(B, N_HEADS, nt // streams),
        in_specs=[
            pl.BlockSpec((None, HEAD_W, streams * tile), lambda b, h, i: (b, h, i)),
            pl.BlockSpec((None, nt, tile, HEAD_W), lambda b, h, i: (b, 0, 0, h)),
            pl.BlockSpec((None, nt, V_ROWS, tile), lambda b, h, i: (b, 0, h, 0)),
            pl.BlockSpec((None, None, META_PAD, HEAD_W), lambda b, h, i: (0, 0, 0, h)),
            pl.BlockSpec((None, None, V_ROWS, META_PAD), lambda b, h, i: (0, 0, h, 0)),
        ],
        out_specs=pl.BlockSpec((None, streams * tile, V_HEAD), lambda b, h, i: (b, i, h)),
        out_shape=jax.ShapeDtypeStruct((B, S, ATTN_WIDTH), BF16),
        scratch_shapes=[pltpu.VMEM((streams, 1, tile), F32),
                        pltpu.VMEM((streams, V_ROWS, tile), F32),
                        pltpu.VMEM((2, streams, tile, tile), F32), pltpu.VMEM((2, streams, 1, tile), F32)],
        compiler_params=pltpu.CompilerParams(dimension_semantics=("parallel", "parallel", "arbitrary"),
                                             vmem_limit_bytes=VMEM_LIMIT),
        name="flash",
    )(q, k, vt, k_meta, vt_meta)


def _decode_body(q_ref, cn_ref, pn_ref, cc_ref, pc_ref, wukt_ref, wuv_ref, o_ref):
    q = q_ref[...]
    t = q.shape[0]
    qa = jnp.concatenate(
        [_mm(q[:, h * HEAD_W:h * HEAD_W + QK_NOPE], wukt_ref[h]) for h in range(N_HEADS)], axis=0).astype(BF16)
    qp = jnp.concatenate([q[:, h * HEAD_W + QK_NOPE:(h + 1) * HEAD_W] for h in range(N_HEADS)], axis=0)
    cc = cc_ref[...].astype(BF16)
    cn = cn_ref[...].astype(BF16)
    s_c = _nt(qa, cc) + _nt(qp, pc_ref[...])
    s_n = _nt(qa, cn) + _nt(qp, pn_ref[...])
    m = jnp.maximum(jnp.max(s_c, axis=-1, keepdims=True), jnp.max(s_n, axis=-1, keepdims=True))
    p_c = jnp.exp2(s_c - m)
    p_n = jnp.exp2(s_n - m)
    l = jnp.sum(p_c, axis=-1, keepdims=True) + jnp.sum(p_n, axis=-1, keepdims=True)
    o_lat = ((_mm(p_c.astype(BF16), cc) + _mm(p_n.astype(BF16), cn)) / l).astype(BF16)
    for h in range(N_HEADS):
        o_ref[:, h * V_HEAD:(h + 1) * V_HEAD] = _mm(o_lat[h * t:(h + 1) * t], wuv_ref[h]).astype(BF16)


def _decode(q, c_new, kpe2_new, cache_c, cache_pe2, wukt, wuv, *, seq):
    R = q.shape[0]
    nb, past, _ = cache_c.shape
    assert nb * seq == R
    return pl.pallas_call(
        _decode_body,
        grid=(nb,),
        in_specs=[
            pl.BlockSpec((seq, N_HEADS * HEAD_W), lambda i: (i, 0)),
            pl.BlockSpec((seq, KV_LORA), lambda i: (i, 0)),
            pl.BlockSpec((seq, LANES), lambda i: (i, 0)),
            pl.BlockSpec((None, past, KV_LORA), lambda i: (i, 0, 0)),
            pl.BlockSpec((None, past, LANES), lambda i: (i, 0, 0)),
            _const_spec(wukt.shape),
            _const_spec(wuv.shape),
        ],
        out_specs=pl.BlockSpec((seq, ATTN_WIDTH), lambda i: (i, 0)),
        out_shape=jax.ShapeDtypeStruct((R, ATTN_WIDTH), BF16),
        compiler_params=pltpu.CompilerParams(dimension_semantics=("parallel",),
                                             vmem_limit_bytes=VMEM_LIMIT),
        name="decode",
    )(q, c_new, kpe2_new, cache_c, cache_pe2, wukt, wuv)


def _pool_seq(hist, u):
    ext = jnp.concatenate([hist, u], axis=0)
    outs = []
    for g, w in enumerate(POOL_WINDOWS):
        e = ext[:, g * POOL_GROUP:(g + 1) * POOL_GROUP]
        s = e
        shift = 1
        while shift < w:
            s = s + pltpu.roll(s, shift, axis=0)
            shift *= 2
        outs.append(s[HIST_ROWS:] * (1.0 / w) - e[HIST_ROWS:])
    return outs


def _merge_body(*refs, n_seq, prompt):
    if prompt:
        (x_ref, o_ref, u_ref, hprev_ref, hmeta_ref, g_ref, wab_ref, wpg_ref, ps_ref, wpb_ref, wout_ref,
         gffn_ref, wup_ref, wdown_ref, gfin_ref, y_ref) = refs
        first = pl.program_id(1) == 0
        hists = [jnp.where(first, hmeta_ref[...], hprev_ref[...])]
    else:
        (x_ref, o_ref, u_ref, hist_ref, g_ref, wab_ref, wpg_ref, ps_ref, wpb_ref, wout_ref,
         gffn_ref, wup_ref, wdown_ref, gfin_ref, y_ref) = refs
        hists = [hist_ref[i] for i in range(n_seq)]

    rows = x_ref.shape[0]
    t = rows // n_seq
    u = u_ref[...]
    per_seq = [_pool_seq(hists[i], u[i * t:(i + 1) * t]) for i in range(n_seq)]
    mixed = []
    for g in range(len(POOL_WINDOWS)):
        pooled = jnp.concatenate([per_seq[i][g] for i in range(n_seq)], axis=0) if n_seq > 1 else per_seq[0][g]
        mixed.append(_mm(pooled.astype(BF16), wpg_ref[g]))
    pm = (jnp.concatenate(mixed, axis=-1) * ps_ref[...]).astype(BF16)
    p = _mm(pm, wpb_ref[...])
    a = _mm(o_ref[...], wab_ref[...])
    gates = g_ref[...].astype(F32)
    mix = (gates[:, :D_MODEL] * a + gates[:, D_MODEL:] * p).astype(BF16)
    h = x_ref[...] + _mm(mix, wout_ref[...])
    hn = _rms(h, gffn_ref[...]).astype(BF16)
    for c in range(D_FF // FF_CHUNK):
        up = jnp.maximum(_mm(hn, wup_ref[:, c * FF_CHUNK:(c + 1) * FF_CHUNK]), 0.0)
        h = h + _mm((up * up).astype(BF16), wdown_ref[c * FF_CHUNK:(c + 1) * FF_CHUNK, :])
    y_ref[...] = _rms(h, gfin_ref[...])


def _merge_ffn(x, o, u, gates, hist_args, w, *, tile, n_seq, prompt):
    G, R, _ = x.shape
    nt = R // tile
    row = lambda width: pl.BlockSpec((None, tile, width), lambda g, t: (g, t, 0))
    if prompt:
        (u_meta,) = hist_args
        per_tile = tile // HIST_ROWS
        hist_specs = [
            pl.BlockSpec((None, HIST_ROWS, D_POOL), lambda g, t: (g, jnp.maximum(t * per_tile - 1, 0), 0)),
            pl.BlockSpec((HIST_ROWS, D_POOL), lambda g, t: (0, 0)),
        ]
        hist_in = [u, u_meta]
    else:
        (hist,) = hist_args
        hist_specs = [pl.BlockSpec((n_seq, HIST_ROWS, D_POOL), lambda g, t: (t, 0, 0))]
        hist_in = [hist]
    weights = (w["w_attn_br"], w["w_pool_grp"], w["pool_scale"], w["w_pool_br"], w["w_out"],
               w["g_ffn"], w["w_up"], w["w_down"], w["g_final"])
    in_specs = ([row(D_MODEL), row(ATTN_WIDTH), row(D_POOL)] + hist_specs + [row(2 * D_MODEL)]
                + [_const_spec(a.shape) for a in weights])
    return pl.pallas_call(
        functools.partial(_merge_body, n_seq=n_seq, prompt=prompt),
        grid=(G, nt),
        in_specs=in_specs,
        out_specs=row(D_MODEL),
        out_shape=jax.ShapeDtypeStruct((G, R, D_MODEL), F32),
        compiler_params=pltpu.CompilerParams(dimension_semantics=("parallel", "parallel"),
                                             vmem_limit_bytes=VMEM_LIMIT),
        name="merge_ffn",
    )(x, o, u, *hist_in, gates, *weights)


def _rope_tables(pos):
    half = QK_ROPE // 2
    inv_freq = jnp.exp(-math.log(ROPE_BASE) * jnp.arange(half, dtype=jnp.float32) / half)
    ang = pos.astype(jnp.float32)[:, None] * inv_freq[None, :]
    c, s = jnp.cos(ang), jnp.sin(ang)
    reps = LANES // QK_ROPE
    return jnp.concatenate([c, c] * reps, axis=-1), jnp.concatenate([-s, s] * reps, axis=-1)


def _prep_weights(l, w_in, g_norm_mix, g_q, g_kv, w_q_up, w_uk, w_uv, w_attn_br, w_pool_grp,
                  pool_scale, w_pool_br, w_out, g_norm_ffn, w_up, w_down, g_final):
    half = QK_ROPE // 2
    i0, i1, i2, i3 = Q_LORA, Q_LORA + KV_LORA, Q_LORA + KV_LORA + QK_ROPE, Q_LORA + KV_LORA + QK_ROPE + D_POOL
    wi = w_in[l]
    w_kr = wi[:, i1:i2]
    w_kr_sw = jnp.concatenate([w_kr[:, half:], w_kr[:, :half]], axis=-1)
    wq = w_q_up[l].reshape(Q_LORA, N_HEADS, QK_NOPE + QK_ROPE)
    wq_r1, wq_r2 = wq[:, :, QK_NOPE:QK_NOPE + half], wq[:, :, QK_NOPE + half:]
    row = lambda v: v.reshape(1, -1).astype(F32)
    w_qup = jnp.concatenate([
        wq[:, :, :QK_NOPE].reshape(Q_LORA, -1),
        jnp.concatenate([wq_r1, wq_r2], axis=-1).reshape(Q_LORA, -1),
        jnp.concatenate([wq_r2, wq_r1], axis=-1).reshape(Q_LORA, -1)], axis=-1).astype(BF16)
    return {
        "g_mix": row(g_norm_mix[l]), "g_q": row(g_q[l]), "g_kv": row(g_kv[l]),
        "w_qlat": wi[:, :i0].astype(BF16),
        "w_kvr": jnp.concatenate([wi[:, i0:i1], w_kr, w_kr, w_kr_sw, w_kr_sw], axis=-1).astype(BF16),
        "w_u": wi[:, i2:i3].astype(BF16),
        "w_g": wi[:, i3:].astype(BF16),
        "w_qup": w_qup,
        "w_qup_t": w_qup[:, :N_HEADS * (QK_NOPE + QK_ROPE)].T,
        "w_uk": w_uk[l].reshape(KV_LORA, N_HEADS * QK_NOPE).astype(BF16),
        "w_uvt": w_uv[l].reshape(KV_LORA, N_HEADS * V_HEAD).T.astype(BF16),
        "w_ukt": jnp.transpose(w_uk[l], (1, 2, 0)).astype(BF16),
        "w_uv3": jnp.transpose(w_uv[l], (1, 0, 2)).astype(BF16),
        "w_attn_br": w_attn_br[l].astype(BF16),
        "w_pool_grp": w_pool_grp[l].astype(BF16),
        "pool_scale": row(pool_scale[l]),
        "w_pool_br": w_pool_br[l].astype(BF16),
        "w_out": w_out[l].astype(BF16),
        "g_ffn": row(g_norm_ffn[l]),
        "w_up": w_up[l].astype(BF16),
        "w_down": w_down[l].astype(BF16),
        "g_final": row(g_final),
    }


def kernel(x_prompt, x_sample, cache_kv_latent, cache_k_rope, cache_pool, meta_tokens, w_in, g_norm_mix, g_q, g_kv, w_q_up, w_uk, w_uv, w_attn_br, w_pool_grp, pool_scale, w_pool_br, w_out, g_norm_ffn, w_up, w_down, g_final):
    B, S, _ = x_prompt.shape
    Bd, T, _ = x_sample.shape
    past = cache_kv_latent.shape[2]
    assert w_in.shape[0] == 1 and S % ROW_TILE == 0 and (Bd * T) % ROW_TILE == 0 and ROW_TILE % T == 0
    assert N_META >= max(POOL_WINDOWS) and POOL_HIST + 1 >= max(POOL_WINDOWS) and T >= POOL_HIST
    w = _prep_weights(0, w_in, g_norm_mix, g_q, g_kv, w_q_up, w_uk, w_uv, w_attn_br, w_pool_grp,
                      pool_scale, w_pool_br, w_out, g_norm_ffn, w_up, w_down, g_final)

    meta = jnp.concatenate([meta_tokens.astype(F32), jnp.zeros((META_PAD - N_META, D_MODEL), F32)], axis=0)
    cos_m, sin_m = _rope_tables(jnp.arange(META_PAD, dtype=jnp.int32))
    pm = _project(meta[None], cos_m, sin_m, w, tile=META_PAD, emit_q=False, emit_kv=True,
                  emit_gates=False, emit_kpe2=False)

    cos_p, sin_p = _rope_tables(N_META + jnp.arange(S, dtype=jnp.int32))
    pp = _project(x_prompt, cos_p, sin_p, w, tile=ROW_TILE, emit_q="transposed", emit_kv=True,
                  emit_gates=True, emit_kpe2=False, lead_rows=N_META)
    o_p = _flash(pp["q"], pp["k"], pp["vt"], pm["k"], pm["vt"])
    y_prompt = _merge_ffn(x_prompt, o_p, pp["u"], pp["gates"], (pm["u"][0],), w,
                          tile=ROW_TILE, n_seq=1, prompt=True)

    rs = Bd * T
    cos_s, sin_s = _rope_tables(jnp.tile(past + jnp.arange(T, dtype=jnp.int32), Bd))
    xs = x_sample.reshape(1, rs, D_MODEL)
    ps = _project(xs, cos_s, sin_s, w, tile=ROW_TILE, emit_q=True, emit_kv=False,
                  emit_gates=True, emit_kpe2=True)
    cache_pe2 = jnp.concatenate([cache_k_rope[0]] * (LANES // QK_ROPE), axis=-1).astype(BF16)
    o_s = _decode(ps["q"][0], ps["c"][0], ps["kpe2"][0], cache_kv_latent[0], cache_pe2,
                  w["w_ukt"], w["w_uv3"], seq=T)
    hist_s = jnp.concatenate([jnp.zeros((Bd, HIST_ROWS - POOL_HIST, D_POOL), F32), cache_pool[0]], axis=1)
    y_sample = _merge_ffn(xs, o_s[None], ps["u"], ps["gates"], (hist_s,), w,
                          tile=ROW_TILE, n_seq=ROW_TILE // T, prompt=False).reshape(Bd, T, D_MODEL)

    def with_meta(m, f):
        return _fill_lead_rows(f, m[0, :N_META])[None]

    c_s = ps["c"].reshape(Bd, T, KV_LORA)
    pe_s = ps["kpe"].reshape(Bd, T, QK_ROPE)
    u_s = ps["u"].reshape(Bd, T, D_POOL)
    return (y_prompt, y_sample,
            with_meta(pm["c"], pp["c"]), with_meta(pm["kpe"], pp["kpe"]), pp["u"][:, -POOL_HIST:][None],
            c_s[None], pe_s[None], u_s[:, -POOL_HIST:][None])
```

```python
import functools
import math

import jax
import jax.numpy as jnp
from jax import lax
from jax.experimental import pallas as pl
from jax.experimental.pallas import tpu as pltpu

F32 = jnp.float32
BF16 = jnp.bfloat16

D_MODEL = 1024
N_HEADS = 8
Q_LORA = 384
KV_LORA = 256
QK_NOPE = 128
QK_ROPE = 64
V_HEAD = 128
ATTN_WIDTH = N_HEADS * V_HEAD
POOL_WINDOWS = (2, 4, 8, 16)
POOL_GROUP = 128
D_POOL = len(POOL_WINDOWS) * POOL_GROUP
POOL_HIST = max(POOL_WINDOWS) - 1
HIST_ROWS = POOL_HIST + 1
D_FF = 4 * D_MODEL
N_META = 16
CHUNK = 64
ROPE_BASE = 10000.0
EPS = 1e-6
SM_SCALE = (QK_NOPE + QK_ROPE) ** -0.5
LOG2E = 1.4426950408889634
NEG_INF = -1e30

LANES = 128
HEAD_W = QK_NOPE + LANES
V_ROWS = V_HEAD + 16
ROW_TILE = 512
META_PAD = 128
Q_STREAMS = 4
FF_CHUNK = 1024
VMEM_LIMIT = 56 * 1024 * 1024


def _nt(a, b):
    return lax.dot_general(a, b, (((1,), (1,)), ((), ())), preferred_element_type=F32)


def _mm(a, b):
    return jnp.dot(a, b, preferred_element_type=F32)


def _rms(x, g):
    return x * lax.rsqrt(jnp.mean(x * x, axis=-1, keepdims=True) + EPS) * g


def _const_spec(shape):
    nd = len(shape)
    return pl.BlockSpec(shape, lambda *_: (0,) * nd, pipeline_mode=pl.Buffered(1))


def _project_body(x_ref, cos_ref, sin_ref, *rest, emit_q, emit_kv, emit_gates, emit_kpe2):
    rest = list(rest)
    cos_t_ref, sin_t_ref = (rest.pop(0), rest.pop(0)) if emit_q == "transposed" else (None, None)
    gmix_ref, gq_ref, gkv_ref, wql_ref, wkvr_ref, wu_ref, wg_ref, wqup_ref, wuk_ref, wuvt_ref = rest[:10]
    outs = rest[10:]
    q_ref = outs.pop(0) if emit_q else None
    k_ref, vt_ref = (outs.pop(0), outs.pop(0)) if emit_kv else (None, None)
    c_ref, kpe_ref, u_ref = outs.pop(0), outs.pop(0), outs.pop(0)
    if len(c_ref.shape) == 3:
        c_ref, kpe_ref = c_ref.at[0], kpe_ref.at[0]
    g_ref = outs.pop(0) if emit_gates else None
    kpe2_ref = outs.pop(0) if emit_kpe2 else None

    rows = x_ref.shape[0]
    xn = _rms(x_ref[...], gmix_ref[...]).astype(BF16)
    cos = cos_ref[...]
    sin = sin_ref[...]
    lane = lax.broadcasted_iota(jnp.int32, (rows, LANES), 1)
    half_masks = (lane < QK_ROPE, lane >= QK_ROPE)

    q_lat = _mm(xn, wql_ref[...]) if emit_q else None

    kvr = _mm(xn, wkvr_ref[...])
    c = _rms(kvr[:, :KV_LORA], gkv_ref[...])
    c_ref[...] = c
    kpe2 = kvr[:, KV_LORA:KV_LORA + LANES] * cos + kvr[:, KV_LORA + LANES:] * sin
    kpe_ref[...] = kpe2[:, :QK_ROPE]
    if emit_kpe2:
        kpe2_ref[...] = kpe2.astype(BF16)
    cb = c.astype(BF16)

    nope_w = N_HEADS * QK_NOPE
    rope_w = N_HEADS * QK_ROPE
    if emit_q == "transposed":
        qn = _rms(q_lat, gq_ref[...]).astype(BF16)
        qall = _nt(wqup_ref[...], qn) * (SM_SCALE * LOG2E)
        half = QK_ROPE // 2
        c_t = cos_t_ref[...]
        s_t = sin_t_ref[...]
        zeros = jnp.zeros((QK_ROPE, rows), BF16)
        for h in range(N_HEADS):
            r1 = qall[nope_w + h * QK_ROPE:nope_w + h * QK_ROPE + half]
            r2 = qall[nope_w + h * QK_ROPE + half:nope_w + (h + 1) * QK_ROPE]
            r = jnp.concatenate([r1 * c_t - r2 * s_t, r1 * s_t + r2 * c_t], axis=0).astype(BF16)
            own = h * HEAD_W + QK_NOPE + (h % 2) * QK_ROPE
            other = h * HEAD_W + QK_NOPE + (1 - h % 2) * QK_ROPE
            q_ref[h * HEAD_W:h * HEAD_W + QK_NOPE, :] = qall[h * QK_NOPE:(h + 1) * QK_NOPE].astype(BF16)
            q_ref[own:own + QK_ROPE, :] = r
            q_ref[other:other + QK_ROPE, :] = zeros
    elif emit_q:
        qn = _rms(q_lat, gq_ref[...]).astype(BF16)
        qall = _mm(qn, wqup_ref[...]) * (SM_SCALE * LOG2E)
        for h in range(N_HEADS):
            j = h // 2
            r = qall[:, nope_w + j * LANES:nope_w + (j + 1) * LANES] * cos
            r = r + qall[:, nope_w + rope_w + j * LANES:nope_w + rope_w + (j + 1) * LANES] * sin
            q_ref[:, h * HEAD_W:h * HEAD_W + QK_NOPE] = qall[:, h * QK_NOPE:(h + 1) * QK_NOPE].astype(BF16)
            q_ref[:, h * HEAD_W + QK_NOPE:(h + 1) * HEAD_W] = jnp.where(half_masks[h % 2], r, 0.0).astype(BF16)

    if emit_kv:
        k_nope = _mm(cb, wuk_ref[...])
        for h in range(N_HEADS):
            k_ref[:, h * HEAD_W:h * HEAD_W + QK_NOPE] = k_nope[:, h * QK_NOPE:(h + 1) * QK_NOPE].astype(BF16)
            k_ref[:, h * HEAD_W + QK_NOPE:(h + 1) * HEAD_W] = jnp.where(half_masks[h % 2], kpe2, 0.0).astype(BF16)
        vt = _nt(wuvt_ref[...], cb)
        extra = V_ROWS - V_HEAD
        ones_rows = jnp.where(lax.broadcasted_iota(jnp.int32, (extra, rows), 0) == 0, 1.0, 0.0).astype(BF16)
        for h in range(N_HEADS):
            vt_ref[h * V_ROWS:h * V_ROWS + V_HEAD, :] = vt[h * V_HEAD:(h + 1) * V_HEAD].astype(BF16)
            vt_ref[h * V_ROWS + V_HEAD:(h + 1) * V_ROWS, :] = ones_rows

    if emit_gates:
        logits = _mm(xn, wg_ref[...])
        g_ref[...] = (1.0 / (1.0 + jnp.exp(-logits))).astype(BF16)

    u_ref[...] = _mm(xn, wu_ref[...])


def _project(x, cos, sin, w, *, tile, emit_q, emit_kv, emit_gates, emit_kpe2, lead_rows=0):
    G, R, _ = x.shape
    nt = R // tile
    assert nt * tile == R
    row = lambda width: pl.BlockSpec((None, tile, width), lambda g, t: (g, t, 0))
    tab = pl.BlockSpec((tile, LANES), lambda g, t: (t, 0))
    q_t = emit_q == "transposed"
    weights = (w["g_mix"], w["g_q"], w["g_kv"], w["w_qlat"], w["w_kvr"], w["w_u"], w["w_g"],
               w["w_qup_t"] if q_t else w["w_qup"], w["w_uk"], w["w_uvt"])
    tables = [cos, sin]
    in_specs = [row(D_MODEL), tab, tab]
    if q_t:
        half = QK_ROPE // 2
        tables += [cos[:, :half].T, sin[:, half:QK_ROPE].T]
        in_specs += [pl.BlockSpec((half, tile), lambda g, t: (0, t))] * 2
    in_specs += [_const_spec(a.shape) for a in weights]
    out_shape, out_specs = [], []

    def add(shape, spec, dtype):
        out_shape.append(jax.ShapeDtypeStruct(shape, dtype))
        out_specs.append(spec)

    if q_t:
        add((G, N_HEADS * HEAD_W, R), pl.BlockSpec((None, N_HEADS * HEAD_W, tile), lambda g, t: (g, 0, t)), BF16)
    elif emit_q:
        add((G, R, N_HEADS * HEAD_W), row(N_HEADS * HEAD_W), BF16)
    if emit_kv:
        add((G, nt, tile, N_HEADS * HEAD_W),
            pl.BlockSpec((None, None, tile, N_HEADS * HEAD_W), lambda g, t: (g, t, 0, 0)), BF16)
        add((G, nt, N_HEADS * V_ROWS, tile),
            pl.BlockSpec((None, None, N_HEADS * V_ROWS, tile), lambda g, t: (g, t, 0, 0)), BF16)
    cache = row if lead_rows == 0 else (lambda width: pl.BlockSpec(
        (pl.Element(1), pl.Element(tile), pl.Element(width)),
        lambda g, t: (g, pl.multiple_of(lead_rows + t * tile, math.gcd(lead_rows, tile)), 0)))
    add((G, lead_rows + R, KV_LORA), cache(KV_LORA), F32)
    add((G, lead_rows + R, QK_ROPE), cache(QK_ROPE), F32)
    add((G, R, D_POOL), row(D_POOL), F32)
    if emit_gates:
        add((G, R, 2 * D_MODEL), row(2 * D_MODEL), BF16)
    if emit_kpe2:
        add((G, R, LANES), row(LANES), BF16)

    body = functools.partial(_project_body, emit_q=emit_q, emit_kv=emit_kv, emit_gates=emit_gates,
                             emit_kpe2=emit_kpe2)
    outs = pl.pallas_call(
        body,
        grid=(G, nt),
        in_specs=in_specs,
        out_specs=out_specs,
        out_shape=out_shape,
        compiler_params=pltpu.CompilerParams(dimension_semantics=("parallel", "parallel"),
                                             vmem_limit_bytes=VMEM_LIMIT),
        name="project",
    )(x, *tables, *weights)
    outs = list(outs)
    res = {}
    if emit_q:
        res["q"] = outs.pop(0)
    if emit_kv:
        res["k"], res["vt"] = outs.pop(0), outs.pop(0)
    res["c"], res["kpe"], res["u"] = outs.pop(0), outs.pop(0), outs.pop(0)
    if emit_gates:
        res["gates"] = outs.pop(0)
    if emit_kpe2:
        res["kpe2"] = outs.pop(0)
    return res


def _fill_lead_body(big_ref, rows_ref, out_ref):
    del big_ref
    out_ref[...] = rows_ref[...]


def _fill_lead_rows(big, rows):
    G, _, W = big.shape
    n = rows.shape[0]
    return pl.pallas_call(
        _fill_lead_body,
        grid=(G,),
        in_specs=[pl.BlockSpec(memory_space=pl.ANY), pl.BlockSpec((n, W), lambda g: (0, 0))],
        out_specs=pl.BlockSpec((None, n, W), lambda g: (g, 0, 0)),
        out_shape=jax.ShapeDtypeStruct(big.shape, big.dtype),
        input_output_aliases={0: 0},
        name="fill_lead_rows",
    )(big, rows)


def _flash_body(qt_ref, k_ref, vt_ref, km_ref, vtm_ref, o_ref, m_ref, acc_ref, s_ref, cm_ref, *, streams):
    tk = k_ref.shape[1]
    base = pl.program_id(2) * streams

    def produce(buf, kb, ss):
        for s in ss:
            s_t = _mm(k_ref[kb], qt_ref[:, s * tk:(s + 1) * tk])
            s_ref[buf, s] = s_t
            cm_ref[buf, s] = jnp.max(s_t, axis=0, keepdims=True)

    def own_columns(buf, s):
        diag = (lax.broadcasted_iota(jnp.int32, (LANES, LANES), 0) // CHUNK
                <= lax.broadcasted_iota(jnp.int32, (LANES, LANES), 1) // CHUNK)
        cols = []
        for j in range(tk // LANES):
            tiles = [s_ref[buf, s, i * LANES:(i + 1) * LANES, j * LANES:(j + 1) * LANES] for i in range(j + 1)]
            tiles[j] = jnp.where(diag, tiles[j], NEG_INF)
            cols.append(jnp.concatenate(tiles, axis=0))
        return cols

    def consume(buf, s, kb, own=False):
        m_old = m_ref[s]
        if own:
            cols = own_columns(buf, s)
            m_blk = jnp.concatenate([jnp.max(c, axis=0, keepdims=True) for c in cols], axis=1)
            m_new = jnp.maximum(m_old, m_blk)
            p_cols = []
            for j, c in enumerate(cols):
                p_j = jnp.exp2(c - m_new[:, j * LANES:(j + 1) * LANES]).astype(BF16)
                if c.shape[0] < tk:
                    p_j = jnp.concatenate([p_j, jnp.zeros((tk - c.shape[0], LANES), BF16)], axis=0)
                p_cols.append(p_j)
            p = jnp.concatenate(p_cols, axis=1)
        else:
            m_new = jnp.maximum(m_old, cm_ref[buf, s])
            p = jnp.exp2(s_ref[buf, s] - m_new).astype(BF16)
        acc_ref[s] = jnp.exp2(m_old - m_new) * acc_ref[s] + _mm(vt_ref[kb], p)
        m_ref[s] = m_new

    everyone = range(streams)

    s_m = _mm(km_ref[...], qt_ref[...])
    produce(0, 0, everyone)
    s_m = jnp.where(lax.broadcasted_iota(jnp.int32, s_m.shape, 0) < N_META, s_m, NEG_INF)
    m_0 = jnp.max(s_m, axis=0, keepdims=True)
    acc_0 = _mm(vtm_ref[...], jnp.exp2(s_m - m_0).astype(BF16))
    for s in everyone:
        m_ref[s] = m_0[:, s * tk:(s + 1) * tk]
        acc_ref[s] = acc_0[:, s * tk:(s + 1) * tk]

    def step(rd, wr, kb):
        lead = 0
        for i in range(streams + lead):
            if i < streams:
                produce(wr, kb + 1, [i])
            if i >= lead:
                consume(rd, i - lead, kb)

    per_trip = 4 if streams % 4 == 0 else 2

    def trip(j, carry):
        for i in range(per_trip):
            step(i % 2, 1 - i % 2, per_trip * j + i)
        return carry

    lax.fori_loop(0, base // per_trip, trip, 0)

    for w in everyone:
        buf = w % 2
        later = list(range(w + 1, streams))
        produce(1 - buf, base + w + 1, later[:1])
        for n, s in enumerate(range(w, streams)):
            produce(1 - buf, base + w + 1, later[n + 1:n + 2])
            consume(buf, s, base + w, own=(s == w))

    for s in everyone:
        acc = acc_ref[s]
        o_ref[s * tk:(s + 1) * tk, :] = (acc[:V_HEAD] / acc[V_HEAD:V_HEAD + 1]).T.astype(BF16)


def _flash(q, k, vt, k_meta, vt_meta):
    B, _, S = q.shape
    nt, tile = k.shape[1], k.shape[2]
    streams = Q_STREAMS
    assert nt % streams == 0 and streams % 2 == 0
    return pl.pallas_call(
        functools.partial(_flash_body, streams=streams),
        grid=(B, N_HEADS, nt // streams),
        in_specs=[
            pl.BlockSpec((None, HEAD_W, streams * tile), lambda b, h, i: (b, h, i)),
            pl.BlockSpec((None, nt, tile, HEAD_W), lambda b, h, i: (b, 0, 0, h)),
            pl.BlockSpec((None, nt, V_ROWS, tile), lambda b, h, i: (b, 0, h, 0)),
            pl.BlockSpec((None, None, META_PAD, HEAD_W), lambda b, h, i: (0, 0, 0, h)),
            pl.BlockSpec((None, None, V_ROWS, META_PAD), lambda b, h, i: (0, 0, h, 0)),
        ],
        out_specs=pl.BlockSpec((None, streams * tile, V_HEAD), lambda b, h, i: (b, i, h)),
        out_shape=jax.ShapeDtypeStruct((B, S, ATTN_WIDTH), BF16),
        scratch_shapes=[pltpu.VMEM((streams, 1, tile), F32),
                        pltpu.VMEM((streams, V_ROWS, tile), F32),
                        pltpu.VMEM((2, streams, tile, tile), F32), pltpu.VMEM((2, streams, 1, tile), F32)],
        compiler_params=pltpu.CompilerParams(dimension_semantics=("parallel", "parallel", "arbitrary"),
                                             vmem_limit_bytes=VMEM_LIMIT),
        name="flash",
    )(q, k, vt, k_meta, vt_meta)


def _decode_body(q_ref, cn_ref, pn_ref, cc_ref, pc_ref, wukt_ref, wuv_ref, o_ref):
    q = q_ref[...]
    t = q.shape[0]
    qa = jnp.concatenate(
        [_mm(q[:, h * HEAD_W:h * HEAD_W + QK_NOPE], wukt_ref[h]) for h in range(N_HEADS)], axis=0).astype(BF16)
    qp = jnp.concatenate([q[:, h * HEAD_W + QK_NOPE:(h + 1) * HEAD_W] for h in range(N_HEADS)], axis=0)
    cc = cc_ref[...].astype(BF16)
    cn = cn_ref[...].astype(BF16)
    s_c = _nt(qa, cc) + _nt(qp, pc_ref[...])
    s_n = _nt(qa, cn) + _nt(qp, pn_ref[...])
    m = jnp.maximum(jnp.max(s_c, axis=-1, keepdims=True), jnp.max(s_n, axis=-1, keepdims=True))
    p_c = jnp.exp2(s_c - m)
    p_n = jnp.exp2(s_n - m)
    l = jnp.sum(p_c, axis=-1, keepdims=True) + jnp.sum(p_n, axis=-1, keepdims=True)
    o_lat = ((_mm(p_c.astype(BF16), cc) + _mm(p_n.astype(BF16), cn)) / l).astype(BF16)
    for h in range(N_HEADS):
        o_ref[:, h * V_HEAD:(h + 1) * V_HEAD] = _mm(o_lat[h * t:(h + 1) * t], wuv_ref[h]).astype(BF16)


def _decode(q, c_new, kpe2_new, cache_c, cache_pe2, wukt, wuv, *, seq):
    R = q.shape[0]
    nb, past, _ = cache_c.shape
    assert nb * seq == R
    return pl.pallas_call(
        _decode_body,
        grid=(nb,),
        in_specs=[
            pl.BlockSpec((seq, N_HEADS * HEAD_W), lambda i: (i, 0)),
            pl.BlockSpec((seq, KV_LORA), lambda i: (i, 0)),
            pl.BlockSpec((seq, LANES), lambda i: (i, 0)),
            pl.BlockSpec((None, past, KV_LORA), lambda i: (i, 0, 0)),
            pl.BlockSpec((None, past, LANES), lambda i: (i, 0, 0)),
            _const_spec(wukt.shape),
            _const_spec(wuv.shape),
        ],
        out_specs=pl.BlockSpec((seq, ATTN_WIDTH), lambda i: (i, 0)),
        out_shape=jax.ShapeDtypeStruct((R, ATTN_WIDTH), BF16),
        compiler_params=pltpu.CompilerParams(dimension_semantics=("parallel",),
                                             vmem_limit_bytes=VMEM_LIMIT),
        name="decode",
    )(q, c_new, kpe2_new, cache_c, cache_pe2, wukt, wuv)


def _pool_seq(hist, u):
    ext = jnp.concatenate([hist, u], axis=0)
    outs = []
    for g, w in enumerate(POOL_WINDOWS):
        e = ext[:, g * POOL_GROUP:(g + 1) * POOL_GROUP]
        s = e
        shift = 1
        while shift < w:
            s = s + pltpu.roll(s, shift, axis=0)
            shift *= 2
        outs.append(s[HIST_ROWS:] * (1.0 / w) - e[HIST_ROWS:])
    return outs


def _merge_body(*refs, n_seq, prompt):
    if prompt:
        (x_ref, o_ref, u_ref, hprev_ref, hmeta_ref, g_ref, wab_ref, wpg_ref, ps_ref, wpb_ref, wout_ref,
         gffn_ref, wup_ref, wdown_ref, gfin_ref, y_ref) = refs
        first = pl.program_id(1) == 0
        hists = [jnp.where(first, hmeta_ref[...], hprev_ref[...])]
    else:
        (x_ref, o_ref, u_ref, hist_ref, g_ref, wab_ref, wpg_ref, ps_ref, wpb_ref, wout_ref,
         gffn_ref, wup_ref, wdown_ref, gfin_ref, y_ref) = refs
        hists = [hist_ref[i] for i in range(n_seq)]

    rows = x_ref.shape[0]
    t = rows // n_seq
    halves = [slice(0, rows // 2), slice(rows // 2, rows)]
    a = [_mm(o_ref[hs, :], wab_ref[...]) for hs in halves]
    u = u_ref[...]
    per_seq = [_pool_seq(hists[i], u[i * t:(i + 1) * t]) for i in range(n_seq)]
    mixed = []
    for g in range(len(POOL_WINDOWS)):
        pooled = jnp.concatenate([per_seq[i][g] for i in range(n_seq)], axis=0) if n_seq > 1 else per_seq[0][g]
        mixed.append(_mm(pooled.astype(BF16), wpg_ref[g]))
    pm = (jnp.concatenate(mixed, axis=-1) * ps_ref[...]).astype(BF16)
    p = [_mm(pm[hs], wpb_ref[...]) for hs in halves]
    h = []
    for i, hs in enumerate(halves):
        gates = g_ref[hs, :].astype(F32)
        mix = (gates[:, :D_MODEL] * a[i] + gates[:, D_MODEL:] * p[i]).astype(BF16)
        h.append(x_ref[hs, :] + _mm(mix, wout_ref[...]))
    hn = [_rms(h_i, gffn_ref[...]).astype(BF16) for h_i in h]
    for c in range(D_FF // FF_CHUNK):
        cols = slice(c * FF_CHUNK, (c + 1) * FF_CHUNK)
        up = [jnp.maximum(_mm(hn_i, wup_ref[:, cols]), 0.0) for hn_i in hn]
        h = [h_i + _mm((up_i * up_i).astype(BF16), wdown_ref[cols, :]) for h_i, up_i in zip(h, up)]
    for hs, h_i in zip(halves, h):
        y_ref[hs, :] = _rms(h_i, gfin_ref[...])


def _merge_ffn(x, o, u, gates, hist_args, w, *, tile, n_seq, prompt):
    G, R, _ = x.shape
    nt = R // tile
    row = lambda width: pl.BlockSpec((None, tile, width), lambda g, t: (g, t, 0))
    if prompt:
        (u_meta,) = hist_args
        per_tile = tile // HIST_ROWS
        hist_specs = [
            pl.BlockSpec((None, HIST_ROWS, D_POOL), lambda g, t: (g, jnp.maximum(t * per_tile - 1, 0), 0)),
            pl.BlockSpec((HIST_ROWS, D_POOL), lambda g, t: (0, 0)),
        ]
        hist_in = [u, u_meta]
    else:
        (hist,) = hist_args
        hist_specs = [pl.BlockSpec((n_seq, HIST_ROWS, D_POOL), lambda g, t: (t, 0, 0))]
        hist_in = [hist]
    weights = (w["w_attn_br"], w["w_pool_grp"], w["pool_scale"], w["w_pool_br"], w["w_out"],
               w["g_ffn"], w["w_up"], w["w_down"], w["g_final"])
    in_specs = ([row(D_MODEL), row(ATTN_WIDTH), row(D_POOL)] + hist_specs + [row(2 * D_MODEL)]
                + [_const_spec(a.shape) for a in weights])
    return pl.pallas_call(
        functools.partial(_merge_body, n_seq=n_seq, prompt=prompt),
        grid=(G, nt),
        in_specs=in_specs,
        out_specs=row(D_MODEL),
        out_shape=jax.ShapeDtypeStruct((G, R, D_MODEL), F32),
        compiler_params=pltpu.CompilerParams(dimension_semantics=("parallel", "parallel"),
                                             vmem_limit_bytes=VMEM_LIMIT),
        name="merge_ffn",
    )(x, o, u, *hist_in, gates, *weights)


def _rope_tables(pos):
    half = QK_ROPE // 2
    inv_freq = jnp.exp(-math.log(ROPE_BASE) * jnp.arange(half, dtype=jnp.float32) / half)
    ang = pos.astype(jnp.float32)[:, None] * inv_freq[None, :]
    c, s = jnp.cos(ang), jnp.sin(ang)
    reps = LANES // QK_ROPE
    return jnp.concatenate([c, c] * reps, axis=-1), jnp.concatenate([-s, s] * reps, axis=-1)


def _prep_weights(l, w_in, g_norm_mix, g_q, g_kv, w_q_up, w_uk, w_uv, w_attn_br, w_pool_grp,
                  pool_scale, w_pool_br, w_out, g_norm_ffn, w_up, w_down, g_final):
    half = QK_ROPE // 2
    i0, i1, i2, i3 = Q_LORA, Q_LORA + KV_LORA, Q_LORA + KV_LORA + QK_ROPE, Q_LORA + KV_LORA + QK_ROPE + D_POOL
    wi = w_in[l]
    w_kr = wi[:, i1:i2]
    w_kr_sw = jnp.concatenate([w_kr[:, half:], w_kr[:, :half]], axis=-1)
    wq = w_q_up[l].reshape(Q_LORA, N_HEADS, QK_NOPE + QK_ROPE)
    wq_r1, wq_r2 = wq[:, :, QK_NOPE:QK_NOPE + half], wq[:, :, QK_NOPE + half:]
    row = lambda v: v.reshape(1, -1).astype(F32)
    w_qup = jnp.concatenate([
        wq[:, :, :QK_NOPE].reshape(Q_LORA, -1),
        jnp.concatenate([wq_r1, wq_r2], axis=-1).reshape(Q_LORA, -1),
        jnp.concatenate([wq_r2, wq_r1], axis=-1).reshape(Q_LORA, -1)], axis=-1).astype(BF16)
    return {
        "g_mix": row(g_norm_mix[l]), "g_q": row(g_q[l]), "g_kv": row(g_kv[l]),
        "w_qlat": wi[:, :i0].astype(BF16),
        "w_kvr": jnp.concatenate([wi[:, i0:i1], w_kr, w_kr, w_kr_sw, w_kr_sw], axis=-1).astype(BF16),
        "w_u": wi[:, i2:i3].astype(BF16),
        "w_g": wi[:, i3:].astype(BF16),
        "w_qup": w_qup,
        "w_qup_t": w_qup[:, :N_HEADS * (QK_NOPE + QK_ROPE)].T,
        "w_uk": w_uk[l].reshape(KV_LORA, N_HEADS * QK_NOPE).astype(BF16),
        "w_uvt": w_uv[l].reshape(KV_LORA, N_HEADS * V_HEAD).T.astype(BF16),
        "w_ukt": jnp.transpose(w_uk[l], (1, 2, 0)).astype(BF16),
        "w_uv3": jnp.transpose(w_uv[l], (1, 0, 2)).astype(BF16),
        "w_attn_br": w_attn_br[l].astype(BF16),
        "w_pool_grp": w_pool_grp[l].astype(BF16),
        "pool_scale": row(pool_scale[l]),
        "w_pool_br": w_pool_br[l].astype(BF16),
        "w_out": w_out[l].astype(BF16),
        "g_ffn": row(g_norm_ffn[l]),
        "w_up": w_up[l].astype(BF16),
        "w_down": w_down[l].astype(BF16),
        "g_final": row(g_final),
    }


def kernel(x_prompt, x_sample, cache_kv_latent, cache_k_rope, cache_pool, meta_tokens, w_in, g_norm_mix, g_q, g_kv, w_q_up, w_uk, w_uv, w_attn_br, w_pool_grp, pool_scale, w_pool_br, w_out, g_norm_ffn, w_up, w_down, g_final):
    B, S, _ = x_prompt.shape
    Bd, T, _ = x_sample.shape
    past = cache_kv_latent.shape[2]
    assert w_in.shape[0] == 1 and S % ROW_TILE == 0 and (Bd * T) % ROW_TILE == 0 and ROW_TILE % T == 0
    assert N_META >= max(POOL_WINDOWS) and POOL_HIST + 1 >= max(POOL_WINDOWS) and T >= POOL_HIST
    w = _prep_weights(0, w_in, g_norm_mix, g_q, g_kv, w_q_up, w_uk, w_uv, w_attn_br, w_pool_grp,
                      pool_scale, w_pool_br, w_out, g_norm_ffn, w_up, w_down, g_final)

    meta = jnp.concatenate([meta_tokens.astype(F32), jnp.zeros((META_PAD - N_META, D_MODEL), F32)], axis=0)
    cos_m, sin_m = _rope_tables(jnp.arange(META_PAD, dtype=jnp.int32))
    pm = _project(meta[None], cos_m, sin_m, w, tile=META_PAD, emit_q=False, emit_kv=True,
                  emit_gates=False, emit_kpe2=False)

    cos_p, sin_p = _rope_tables(N_META + jnp.arange(S, dtype=jnp.int32))
    pp = _project(x_prompt, cos_p, sin_p, w, tile=ROW_TILE, emit_q="transposed", emit_kv=True,
                  emit_gates=True, emit_kpe2=False, lead_rows=N_META)
    o_p = _flash(pp["q"], pp["k"], pp["vt"], pm["k"], pm["vt"])
    y_prompt = _merge_ffn(x_prompt, o_p, pp["u"], pp["gates"], (pm["u"][0],), w,
                          tile=ROW_TILE, n_seq=1, prompt=True)

    rs = Bd * T
    cos_s, sin_s = _rope_tables(jnp.tile(past + jnp.arange(T, dtype=jnp.int32), Bd))
    xs = x_sample.reshape(1, rs, D_MODEL)
    ps = _project(xs, cos_s, sin_s, w, tile=ROW_TILE, emit_q=True, emit_kv=False,
                  emit_gates=True, emit_kpe2=True)
    cache_pe2 = jnp.concatenate([cache_k_rope[0]] * (LANES // QK_ROPE), axis=-1).astype(BF16)
    o_s = _decode(ps["q"][0], ps["c"][0], ps["kpe2"][0], cache_kv_latent[0], cache_pe2,
                  w["w_ukt"], w["w_uv3"], seq=T)
    hist_s = jnp.concatenate([jnp.zeros((Bd, HIST_ROWS - POOL_HIST, D_POOL), F32), cache_pool[0]], axis=1)
    y_sample = _merge_ffn(xs, o_s[None], ps["u"], ps["gates"], (hist_s,), w,
                          tile=ROW_TILE, n_seq=ROW_TILE // T, prompt=False).reshape(Bd, T, D_MODEL)

    def with_meta(m, f):
        return _fill_lead_rows(f, m[0, :N_META])[None]

    c_s = ps["c"].reshape(Bd, T, KV_LORA)
    pe_s = ps["kpe"].reshape(Bd, T, QK_ROPE)
    u_s = ps["u"].reshape(Bd, T, D_POOL)
    return (y_prompt, y_sample,
            with_meta(pm["c"], pp["c"]), with_meta(pm["kpe"], pp["kpe"]), pp["u"][:, -POOL_HIST:][None],
            c_s[None], pe_s[None], u_s[:, -POOL_HIST:][None])
```

```python
import functools
import math

import jax
import jax.numpy as jnp
from jax import lax
from jax.experimental import pallas as pl
from jax.experimental.pallas import tpu as pltpu

F32 = jnp.float32
BF16 = jnp.bfloat16

D_MODEL = 1024
N_HEADS = 8
Q_LORA = 384
KV_LORA = 256
QK_NOPE = 128
QK_ROPE = 64
V_HEAD = 128
ATTN_WIDTH = N_HEADS * V_HEAD
POOL_WINDOWS = (2, 4, 8, 16)
POOL_GROUP = 128
D_POOL = len(POOL_WINDOWS) * POOL_GROUP
POOL_HIST = max(POOL_WINDOWS) - 1
HIST_ROWS = POOL_HIST + 1
D_FF = 4 * D_MODEL
N_META = 16
CHUNK = 64
ROPE_BASE = 10000.0
EPS = 1e-6
SM_SCALE = (QK_NOPE + QK_ROPE) ** -0.5
LOG2E = 1.4426950408889634
NEG_INF = -1e30

LANES = 128
HEAD_W = QK_NOPE + LANES
V_ROWS = V_HEAD + 16
ROW_TILE = 512
META_PAD = 128
Q_STREAMS = 4
FF_CHUNK = 1024
VMEM_LIMIT = 56 * 1024 * 1024


def _nt(a, b):
    return lax.dot_general(a, b, (((1,), (1,)), ((), ())), preferred_element_type=F32)


def _mm(a, b):
    return jnp.dot(a, b, preferred_element_type=F32)


def _rms(x, g):
    return x * lax.rsqrt(jnp.mean(x * x, axis=-1, keepdims=True) + EPS) * g


def _const_spec(shape):
    nd = len(shape)
    return pl.BlockSpec(shape, lambda *_: (0,) * nd, pipeline_mode=pl.Buffered(1))


def _project_body(x_ref, cos_ref, sin_ref, *rest, emit_q, emit_kv, emit_gates, emit_kpe2):
    rest = list(rest)
    cos_t_ref, sin_t_ref = (rest.pop(0), rest.pop(0)) if emit_q == "transposed" else (None, None)
    gmix_ref, gq_ref, gkv_ref, wql_ref, wkvr_ref, wu_ref, wg_ref, wqup_ref, wuk_ref, wuvt_ref = rest[:10]
    outs = rest[10:]
    q_ref = outs.pop(0) if emit_q else None
    k_ref, vt_ref = (outs.pop(0), outs.pop(0)) if emit_kv else (None, None)
    c_ref, kpe_ref, u_ref = outs.pop(0), outs.pop(0), outs.pop(0)
    if len(c_ref.shape) == 3:
        c_ref, kpe_ref = c_ref.at[0], kpe_ref.at[0]
    g_ref = outs.pop(0) if emit_gates else None
    kpe2_ref = outs.pop(0) if emit_kpe2 else None

    rows = x_ref.shape[0]
    xn = _rms(x_ref[...], gmix_ref[...]).astype(BF16)
    cos = cos_ref[...]
    sin = sin_ref[...]
    lane = lax.broadcasted_iota(jnp.int32, (rows, LANES), 1)
    half_masks = (lane < QK_ROPE, lane >= QK_ROPE)

    q_lat = _mm(xn, wql_ref[...]) if emit_q else None

    kvr = _mm(xn, wkvr_ref[...])
    c = _rms(kvr[:, :KV_LORA], gkv_ref[...])
    c_ref[...] = c
    kpe2 = kvr[:, KV_LORA:KV_LORA + LANES] * cos + kvr[:, KV_LORA + LANES:] * sin
    kpe_ref[...] = kpe2[:, :QK_ROPE]
    if emit_kpe2:
        kpe2_ref[...] = kpe2.astype(BF16)
    cb = c.astype(BF16)

    nope_w = N_HEADS * QK_NOPE
    rope_w = N_HEADS * QK_ROPE
    if emit_q == "transposed":
        qn = _rms(q_lat, gq_ref[...]).astype(BF16)
        qall = _nt(wqup_ref[...], qn) * (SM_SCALE * LOG2E)
        half = QK_ROPE // 2
        c_t = cos_t_ref[...]
        s_t = sin_t_ref[...]
        zeros = jnp.zeros((QK_ROPE, rows), BF16)
        for h in range(N_HEADS):
            r1 = qall[nope_w + h * QK_ROPE:nope_w + h * QK_ROPE + half]
            r2 = qall[nope_w + h * QK_ROPE + half:nope_w + (h + 1) * QK_ROPE]
            r = jnp.concatenate([r1 * c_t - r2 * s_t, r1 * s_t + r2 * c_t], axis=0).astype(BF16)
            own = h * HEAD_W + QK_NOPE + (h % 2) * QK_ROPE
            other = h * HEAD_W + QK_NOPE + (1 - h % 2) * QK_ROPE
            q_ref[h * HEAD_W:h * HEAD_W + QK_NOPE, :] = qall[h * QK_NOPE:(h + 1) * QK_NOPE].astype(BF16)
            q_ref[own:own + QK_ROPE, :] = r
            q_ref[other:other + QK_ROPE, :] = zeros
    elif emit_q:
        qn = _rms(q_lat, gq_ref[...]).astype(BF16)
        qall = _mm(qn, wqup_ref[...]) * (SM_SCALE * LOG2E)
        for h in range(N_HEADS):
            j = h // 2
            r = qall[:, nope_w + j * LANES:nope_w + (j + 1) * LANES] * cos
            r = r + qall[:, nope_w + rope_w + j * LANES:nope_w + rope_w + (j + 1) * LANES] * sin
            q_ref[:, h * HEAD_W:h * HEAD_W + QK_NOPE] = qall[:, h * QK_NOPE:(h + 1) * QK_NOPE].astype(BF16)
            q_ref[:, h * HEAD_W + QK_NOPE:(h + 1) * HEAD_W] = jnp.where(half_masks[h % 2], r, 0.0).astype(BF16)

    if emit_kv:
        k_nope = _mm(cb, wuk_ref[...])
        for h in range(N_HEADS):
            k_ref[:, h * HEAD_W:h * HEAD_W + QK_NOPE] = k_nope[:, h * QK_NOPE:(h + 1) * QK_NOPE].astype(BF16)
            k_ref[:, h * HEAD_W + QK_NOPE:(h + 1) * HEAD_W] = jnp.where(half_masks[h % 2], kpe2, 0.0).astype(BF16)
        vt = _nt(wuvt_ref[...], cb)
        extra = V_ROWS - V_HEAD
        ones_rows = jnp.where(lax.broadcasted_iota(jnp.int32, (extra, rows), 0) == 0, 1.0, 0.0).astype(BF16)
        for h in range(N_HEADS):
            vt_ref[h * V_ROWS:h * V_ROWS + V_HEAD, :] = vt[h * V_HEAD:(h + 1) * V_HEAD].astype(BF16)
            vt_ref[h * V_ROWS + V_HEAD:(h + 1) * V_ROWS, :] = ones_rows

    if emit_gates:
        logits = _mm(xn, wg_ref[...])
        g_ref[...] = (1.0 / (1.0 + jnp.exp(-logits))).astype(BF16)

    u_ref[...] = _mm(xn, wu_ref[...])


def _project(x, rope, w, *, tile, emit_q, emit_kv, emit_gates, emit_kpe2, lead_rows=0):
    cos, sin, cos_t, sin_t = rope
    G, R, _ = x.shape
    nt = R // tile
    assert nt * tile == R
    row = lambda width: pl.BlockSpec((None, tile, width), lambda g, t: (g, t, 0))
    tab = pl.BlockSpec((tile, LANES), lambda g, t: (t, 0))
    q_t = emit_q == "transposed"
    weights = (w["g_mix"], w["g_q"], w["g_kv"], w["w_qlat"], w["w_kvr"], w["w_u"], w["w_g"],
               w["w_qup_t"] if q_t else w["w_qup"], w["w_uk"], w["w_uvt"])
    tables = [cos, sin]
    in_specs = [row(D_MODEL), tab, tab]
    if q_t:
        tables += [cos_t, sin_t]
        in_specs += [pl.BlockSpec((QK_ROPE // 2, tile), lambda g, t: (0, t))] * 2
    in_specs += [_const_spec(a.shape) for a in weights]
    out_shape, out_specs = [], []

    def add(shape, spec, dtype):
        out_shape.append(jax.ShapeDtypeStruct(shape, dtype))
        out_specs.append(spec)

    if q_t:
        add((G, N_HEADS * HEAD_W, R), pl.BlockSpec((None, N_HEADS * HEAD_W, tile), lambda g, t: (g, 0, t)), BF16)
    elif emit_q:
        add((G, R, N_HEADS * HEAD_W), row(N_HEADS * HEAD_W), BF16)
    if emit_kv:
        add((G, nt, tile, N_HEADS * HEAD_W),
            pl.BlockSpec((None, None, tile, N_HEADS * HEAD_W), lambda g, t: (g, t, 0, 0)), BF16)
        add((G, nt, N_HEADS * V_ROWS, tile),
            pl.BlockSpec((None, None, N_HEADS * V_ROWS, tile), lambda g, t: (g, t, 0, 0)), BF16)
    cache = row if lead_rows == 0 else (lambda width: pl.BlockSpec(
        (pl.Element(1), pl.Element(tile), pl.Element(width)),
        lambda g, t: (g, pl.multiple_of(lead_rows + t * tile, math.gcd(lead_rows, tile)), 0)))
    add((G, lead_rows + R, KV_LORA), cache(KV_LORA), F32)
    add((G, lead_rows + R, QK_ROPE), cache(QK_ROPE), F32)
    add((G, R, D_POOL), row(D_POOL), F32)
    if emit_gates:
        add((G, R, 2 * D_MODEL), row(2 * D_MODEL), BF16)
    if emit_kpe2:
        add((G, R, LANES), row(LANES), BF16)

    body = functools.partial(_project_body, emit_q=emit_q, emit_kv=emit_kv, emit_gates=emit_gates,
                             emit_kpe2=emit_kpe2)
    outs = pl.pallas_call(
        body,
        grid=(G, nt),
        in_specs=in_specs,
        out_specs=out_specs,
        out_shape=out_shape,
        compiler_params=pltpu.CompilerParams(dimension_semantics=("parallel", "parallel"),
                                             vmem_limit_bytes=VMEM_LIMIT),
        name="project",
    )(x, *tables, *weights)
    outs = list(outs)
    res = {}
    if emit_q:
        res["q"] = outs.pop(0)
    if emit_kv:
        res["k"], res["vt"] = outs.pop(0), outs.pop(0)
    res["c"], res["kpe"], res["u"] = outs.pop(0), outs.pop(0), outs.pop(0)
    if emit_gates:
        res["gates"] = outs.pop(0)
    if emit_kpe2:
        res["kpe2"] = outs.pop(0)
    return res


def _fill_lead_body(big_ref, rows_ref, out_ref):
    del big_ref
    out_ref[...] = rows_ref[...]


def _fill_lead_rows(big, rows):
    G, _, W = big.shape
    n = rows.shape[0]
    return pl.pallas_call(
        _fill_lead_body,
        grid=(G,),
        in_specs=[pl.BlockSpec(memory_space=pl.ANY), pl.BlockSpec((n, W), lambda g: (0, 0))],
        out_specs=pl.BlockSpec((None, n, W), lambda g: (g, 0, 0)),
        out_shape=jax.ShapeDtypeStruct(big.shape, big.dtype),
        input_output_aliases={0: 0},
        name="fill_lead_rows",
    )(big, rows)


def _flash_body(qt_ref, k_ref, vt_ref, km_ref, vtm_ref, o_ref, m_ref, acc_ref, s_ref, cm_ref, *, streams):
    tk = k_ref.shape[1]
    base = pl.program_id(2) * streams

    def produce(buf, kb, ss):
        for s in ss:
            s_t = _mm(k_ref[kb], qt_ref[:, s * tk:(s + 1) * tk])
            s_ref[buf, s] = s_t
            cm_ref[buf, s] = jnp.max(s_t, axis=0, keepdims=True)

    def own_columns(buf, s):
        diag = (lax.broadcasted_iota(jnp.int32, (LANES, LANES), 0) // CHUNK
                <= lax.broadcasted_iota(jnp.int32, (LANES, LANES), 1) // CHUNK)
        cols = []
        for j in range(tk // LANES):
            tiles = [s_ref[buf, s, i * LANES:(i + 1) * LANES, j * LANES:(j + 1) * LANES] for i in range(j + 1)]
            tiles[j] = jnp.where(diag, tiles[j], NEG_INF)
            cols.append(jnp.concatenate(tiles, axis=0))
        return cols

    def consume(buf, s, kb, own=False):
        m_old = m_ref[s]
        if own:
            cols = own_columns(buf, s)
            m_blk = jnp.concatenate([jnp.max(c, axis=0, keepdims=True) for c in cols], axis=1)
            m_new = jnp.maximum(m_old, m_blk)
            p_cols = []
            for j, c in enumerate(cols):
                p_j = jnp.exp2(c - m_new[:, j * LANES:(j + 1) * LANES]).astype(BF16)
                if c.shape[0] < tk:
                    p_j = jnp.concatenate([p_j, jnp.zeros((tk - c.shape[0], LANES), BF16)], axis=0)
                p_cols.append(p_j)
            p = jnp.concatenate(p_cols, axis=1)
        else:
            m_new = jnp.maximum(m_old, cm_ref[buf, s])
            p = jnp.exp2(s_ref[buf, s] - m_new).astype(BF16)
        acc_ref[s] = jnp.exp2(m_old - m_new) * acc_ref[s] + _mm(vt_ref[kb], p)
        m_ref[s] = m_new

    everyone = range(streams)

    s_m = _mm(km_ref[...], qt_ref[...])
    produce(0, 0, everyone)
    s_m = jnp.where(lax.broadcasted_iota(jnp.int32, s_m.shape, 0) < N_META, s_m, NEG_INF)
    m_0 = jnp.max(s_m, axis=0, keepdims=True)
    acc_0 = _mm(vtm_ref[...], jnp.exp2(s_m - m_0).astype(BF16))
    for s in everyone:
        m_ref[s] = m_0[:, s * tk:(s + 1) * tk]
        acc_ref[s] = acc_0[:, s * tk:(s + 1) * tk]

    def step(rd, wr, kb):
        lead = 0
        for i in range(streams + lead):
            if i < streams:
                produce(wr, kb + 1, [i])
            if i >= lead:
                consume(rd, i - lead, kb)

    per_trip = 4 if streams % 4 == 0 else 2

    def trip(j, carry):
        for i in range(per_trip):
            step(i % 2, 1 - i % 2, per_trip * j + i)
        return carry

    lax.fori_loop(0, base // per_trip, trip, 0)

    for w in everyone:
        buf = w % 2
        later = list(range(w + 1, streams))
        produce(1 - buf, base + w + 1, later[:1])
        for n, s in enumerate(range(w, streams)):
            produce(1 - buf, base + w + 1, later[n + 1:n + 2])
            consume(buf, s, base + w, own=(s == w))

    for s in everyone:
        acc = acc_ref[s]
        o_ref[s * tk:(s + 1) * tk, :] = (acc[:V_HEAD] / acc[V_HEAD:V_HEAD + 1]).T.astype(BF16)


def _flash(q, k, vt, k_meta, vt_meta):
    B, _, S = q.shape
    nt, tile = k.shape[1], k.shape[2]
    streams = Q_STREAMS
    assert nt % streams == 0 and streams % 2 == 0
    return pl.pallas_call(
        functools.partial(_flash_body, streams=streams),
        grid=(B, N_HEADS, nt // streams),
        in_specs=[
            pl.BlockSpec((None, HEAD_W, streams * tile), lambda b, h, i: (b, h, i)),
            pl.BlockSpec((None, nt, tile, HEAD_W), lambda b, h, i: (b, 0, 0, h)),
            pl.BlockSpec((None, nt, V_ROWS, tile), lambda b, h, i: (b, 0, h, 0)),
            pl.BlockSpec((None, None, META_PAD, HEAD_W), lambda b, h, i: (0, 0, 0, h)),
            pl.BlockSpec((None, None, V_ROWS, META_PAD), lambda b, h, i: (0, 0, h, 0)),
        ],
        out_specs=pl.BlockSpec((None, streams * tile, V_HEAD), lambda b, h, i: (b, i, h)),
        out_shape=jax.ShapeDtypeStruct((B, S, ATTN_WIDTH), BF16),
        scratch_shapes=[pltpu.VMEM((streams, 1, tile), F32),
                        pltpu.VMEM((streams, V_ROWS, tile), F32),
                        pltpu.VMEM((2, streams, tile, tile), F32), pltpu.VMEM((2, streams, 1, tile), F32)],
        compiler_params=pltpu.CompilerParams(dimension_semantics=("parallel", "parallel", "arbitrary"),
                                             vmem_limit_bytes=VMEM_LIMIT),
        name="flash",
    )(q, k, vt, k_meta, vt_meta)


def _decode_body(q_ref, cn_ref, pn_ref, cc_ref, pc_ref, wukt_ref, wuv_ref, o_ref):
    q = q_ref[...]
    t = q.shape[0]
    qa = jnp.concatenate(
        [_mm(q[:, h * HEAD_W:h * HEAD_W + QK_NOPE], wukt_ref[h]) for h in range(N_HEADS)], axis=0).astype(BF16)
    qp = jnp.concatenate([q[:, h * HEAD_W + QK_NOPE:(h + 1) * HEAD_W] for h in range(N_HEADS)], axis=0)
    cc = cc_ref[...].astype(BF16)
    cn = cn_ref[...].astype(BF16)
    s_c = _nt(qa, cc) + _nt(qp, pc_ref[...])
    s_n = _nt(qa, cn) + _nt(qp, pn_ref[...])
    m = jnp.maximum(jnp.max(s_c, axis=-1, keepdims=True), jnp.max(s_n, axis=-1, keepdims=True))
    p_c = jnp.exp2(s_c - m)
    p_n = jnp.exp2(s_n - m)
    l = jnp.sum(p_c, axis=-1, keepdims=True) + jnp.sum(p_n, axis=-1, keepdims=True)
    o_lat = ((_mm(p_c.astype(BF16), cc) + _mm(p_n.astype(BF16), cn)) / l).astype(BF16)
    for h in range(N_HEADS):
        o_ref[:, h * V_HEAD:(h + 1) * V_HEAD] = _mm(o_lat[h * t:(h + 1) * t], wuv_ref[h]).astype(BF16)


def _decode(q, c_new, kpe2_new, cache_c, cache_pe2, wukt, wuv, *, seq):
    R = q.shape[0]
    nb, past, _ = cache_c.shape
    assert nb * seq == R
    return pl.pallas_call(
        _decode_body,
        grid=(nb,),
        in_specs=[
            pl.BlockSpec((seq, N_HEADS * HEAD_W), lambda i: (i, 0)),
            pl.BlockSpec((seq, KV_LORA), lambda i: (i, 0)),
            pl.BlockSpec((seq, LANES), lambda i: (i, 0)),
            pl.BlockSpec((None, past, KV_LORA), lambda i: (i, 0, 0)),
            pl.BlockSpec((None, past, LANES), lambda i: (i, 0, 0)),
            _const_spec(wukt.shape),
            _const_spec(wuv.shape),
        ],
        out_specs=pl.BlockSpec((seq, ATTN_WIDTH), lambda i: (i, 0)),
        out_shape=jax.ShapeDtypeStruct((R, ATTN_WIDTH), BF16),
        compiler_params=pltpu.CompilerParams(dimension_semantics=("parallel",),
                                             vmem_limit_bytes=VMEM_LIMIT),
        name="decode",
    )(q, c_new, kpe2_new, cache_c, cache_pe2, wukt, wuv)


def _pool_seq(hist, u):
    ext = jnp.concatenate([hist, u], axis=0)
    outs = []
    for g, w in enumerate(POOL_WINDOWS):
        e = ext[:, g * POOL_GROUP:(g + 1) * POOL_GROUP]
        s = e
        shift = 1
        while shift < w:
            s = s + pltpu.roll(s, shift, axis=0)
            shift *= 2
        outs.append(s[HIST_ROWS:] * (1.0 / w) - e[HIST_ROWS:])
    return outs


def _merge_body(*refs, n_seq, prompt):
    if prompt:
        (x_ref, o_ref, u_ref, hprev_ref, hmeta_ref, g_ref, wab_ref, wpg_ref, ps_ref, wpb_ref, wout_ref,
         gffn_ref, wup_ref, wdown_ref, gfin_ref, y_ref) = refs
        first = pl.program_id(1) == 0
        hists = [jnp.where(first, hmeta_ref[...], hprev_ref[...])]
    else:
        (x_ref, o_ref, u_ref, hist_ref, g_ref, wab_ref, wpg_ref, ps_ref, wpb_ref, wout_ref,
         gffn_ref, wup_ref, wdown_ref, gfin_ref, y_ref) = refs
        hists = [hist_ref[i] for i in range(n_seq)]

    rows = x_ref.shape[0]
    t = rows // n_seq
    halves = [slice(0, rows // 2), slice(rows // 2, rows)]
    a = [_mm(o_ref[hs, :], wab_ref[...]) for hs in halves]
    u = u_ref[...]
    per_seq = [_pool_seq(hists[i], u[i * t:(i + 1) * t]) for i in range(n_seq)]
    mixed = []
    for g in range(len(POOL_WINDOWS)):
        pooled = jnp.concatenate([per_seq[i][g] for i in range(n_seq)], axis=0) if n_seq > 1 else per_seq[0][g]
        mixed.append(_mm(pooled.astype(BF16), wpg_ref[g]))
    pm = (jnp.concatenate(mixed, axis=-1) * ps_ref[...]).astype(BF16)
    p = [_mm(pm[hs], wpb_ref[...]) for hs in halves]
    h = []
    for i, hs in enumerate(halves):
        gates = g_ref[hs, :].astype(F32)
        mix = (gates[:, :D_MODEL] * a[i] + gates[:, D_MODEL:] * p[i]).astype(BF16)
        h.append(x_ref[hs, :] + _mm(mix, wout_ref[...]))
    hn = [_rms(h_i, gffn_ref[...]).astype(BF16) for h_i in h]
    for c in range(D_FF // FF_CHUNK):
        cols = slice(c * FF_CHUNK, (c + 1) * FF_CHUNK)
        up = [jnp.maximum(_mm(hn_i, wup_ref[:, cols]), 0.0) for hn_i in hn]
        h = [h_i + _mm((up_i * up_i).astype(BF16), wdown_ref[cols, :]) for h_i, up_i in zip(h, up)]
    for hs, h_i in zip(halves, h):
        y_ref[hs, :] = _rms(h_i, gfin_ref[...])


def _merge_ffn(x, o, u, gates, hist_args, w, *, tile, n_seq, prompt):
    G, R, _ = x.shape
    nt = R // tile
    row = lambda width: pl.BlockSpec((None, tile, width), lambda g, t: (g, t, 0))
    if prompt:
        (u_meta,) = hist_args
        per_tile = tile // HIST_ROWS
        hist_specs = [
            pl.BlockSpec((None, HIST_ROWS, D_POOL), lambda g, t: (g, jnp.maximum(t * per_tile - 1, 0), 0)),
            pl.BlockSpec((HIST_ROWS, D_POOL), lambda g, t: (0, 0)),
        ]
        hist_in = [u, u_meta]
    else:
        (hist,) = hist_args
        hist_specs = [pl.BlockSpec((n_seq, HIST_ROWS, D_POOL), lambda g, t: (t, 0, 0))]
        hist_in = [hist]
    weights = (w["w_attn_br"], w["w_pool_grp"], w["pool_scale"], w["w_pool_br"], w["w_out"],
               w["g_ffn"], w["w_up"], w["w_down"], w["g_final"])
    in_specs = ([row(D_MODEL), row(ATTN_WIDTH), row(D_POOL)] + hist_specs + [row(2 * D_MODEL)]
                + [_const_spec(a.shape) for a in weights])
    return pl.pallas_call(
        functools.partial(_merge_body, n_seq=n_seq, prompt=prompt),
        grid=(G, nt),
        in_specs=in_specs,
        out_specs=row(D_MODEL),
        out_shape=jax.ShapeDtypeStruct((G, R, D_MODEL), F32),
        compiler_params=pltpu.CompilerParams(dimension_semantics=("parallel", "parallel"),
                                             vmem_limit_bytes=VMEM_LIMIT),
        name="merge_ffn",
    )(x, o, u, *hist_in, gates, *weights)


def _rope_tables(base_pos, off_pos):
    half = QK_ROPE // 2
    inv_freq = jnp.exp(-math.log(ROPE_BASE) * jnp.arange(half, dtype=jnp.float32) / half)
    reps = LANES // half
    freq = jnp.tile(inv_freq, reps)
    sign = jnp.tile(jnp.concatenate([-jnp.ones(half, F32), jnp.ones(half, F32)]), reps // 2)
    a = base_pos.astype(jnp.float32)[:, None, None] * freq
    b = off_pos.astype(jnp.float32)[None, :, None] * freq
    cos = (jnp.cos(a) * jnp.cos(b) - jnp.sin(a) * jnp.sin(b)).reshape(-1, LANES)
    sin = ((jnp.sin(a) * jnp.cos(b) + jnp.cos(a) * jnp.sin(b)) * sign).reshape(-1, LANES)
    a_t = inv_freq[:, None, None] * base_pos.astype(jnp.float32)[None, :, None]
    b_t = inv_freq[:, None, None] * off_pos.astype(jnp.float32)[None, None, :]
    cos_t = (jnp.cos(a_t) * jnp.cos(b_t) - jnp.sin(a_t) * jnp.sin(b_t)).reshape(half, -1)
    sin_t = (jnp.sin(a_t) * jnp.cos(b_t) + jnp.cos(a_t) * jnp.sin(b_t)).reshape(half, -1)
    return cos, sin, cos_t, sin_t


def _prep_weights(l, w_in, g_norm_mix, g_q, g_kv, w_q_up, w_uk, w_uv, w_attn_br, w_pool_grp,
                  pool_scale, w_pool_br, w_out, g_norm_ffn, w_up, w_down, g_final):
    half = QK_ROPE // 2
    i0, i1, i2, i3 = Q_LORA, Q_LORA + KV_LORA, Q_LORA + KV_LORA + QK_ROPE, Q_LORA + KV_LORA + QK_ROPE + D_POOL
    wi = w_in[l]
    w_kr = wi[:, i1:i2]
    w_kr_sw = jnp.concatenate([w_kr[:, half:], w_kr[:, :half]], axis=-1)
    wq = w_q_up[l].reshape(Q_LORA, N_HEADS, QK_NOPE + QK_ROPE)
    wq_r1, wq_r2 = wq[:, :, QK_NOPE:QK_NOPE + half], wq[:, :, QK_NOPE + half:]
    row = lambda v: v.reshape(1, -1).astype(F32)
    w_qup = jnp.concatenate([
        wq[:, :, :QK_NOPE].reshape(Q_LORA, -1),
        jnp.concatenate([wq_r1, wq_r2], axis=-1).reshape(Q_LORA, -1),
        jnp.concatenate([wq_r2, wq_r1], axis=-1).reshape(Q_LORA, -1)], axis=-1).astype(BF16)
    return {
        "g_mix": row(g_norm_mix[l]), "g_q": row(g_q[l]), "g_kv": row(g_kv[l]),
        "w_qlat": wi[:, :i0].astype(BF16),
        "w_kvr": jnp.concatenate([wi[:, i0:i1], w_kr, w_kr, w_kr_sw, w_kr_sw], axis=-1).astype(BF16),
        "w_u": wi[:, i2:i3].astype(BF16),
        "w_g": wi[:, i3:].astype(BF16),
        "w_qup": w_qup,
        "w_qup_t": w_qup[:, :N_HEADS * (QK_NOPE + QK_ROPE)].T,
        "w_uk": w_uk[l].reshape(KV_LORA, N_HEADS * QK_NOPE).astype(BF16),
        "w_uvt": w_uv[l].reshape(KV_LORA, N_HEADS * V_HEAD).T.astype(BF16),
        "w_ukt": jnp.transpose(w_uk[l], (1, 2, 0)).astype(BF16),
        "w_uv3": jnp.transpose(w_uv[l], (1, 0, 2)).astype(BF16),
        "w_attn_br": w_attn_br[l].astype(BF16),
        "w_pool_grp": w_pool_grp[l].astype(BF16),
        "pool_scale": row(pool_scale[l]),
        "w_pool_br": w_pool_br[l].astype(BF16),
        "w_out": w_out[l].astype(BF16),
        "g_ffn": row(g_norm_ffn[l]),
        "w_up": w_up[l].astype(BF16),
        "w_down": w_down[l].astype(BF16),
        "g_final": row(g_final),
    }


def kernel(x_prompt, x_sample, cache_kv_latent, cache_k_rope, cache_pool, meta_tokens, w_in, g_norm_mix, g_q, g_kv, w_q_up, w_uk, w_uv, w_attn_br, w_pool_grp, pool_scale, w_pool_br, w_out, g_norm_ffn, w_up, w_down, g_final):
    B, S, _ = x_prompt.shape
    Bd, T, _ = x_sample.shape
    past = cache_kv_latent.shape[2]
    assert w_in.shape[0] == 1 and S % ROW_TILE == 0 and (Bd * T) % ROW_TILE == 0 and ROW_TILE % T == 0
    assert N_META >= max(POOL_WINDOWS) and POOL_HIST + 1 >= max(POOL_WINDOWS) and T >= POOL_HIST
    w = _prep_weights(0, w_in, g_norm_mix, g_q, g_kv, w_q_up, w_uk, w_uv, w_attn_br, w_pool_grp,
                      pool_scale, w_pool_br, w_out, g_norm_ffn, w_up, w_down, g_final)

    meta = jnp.concatenate([meta_tokens.astype(F32), jnp.zeros((META_PAD - N_META, D_MODEL), F32)], axis=0)
    iota = functools.partial(jnp.arange, dtype=jnp.int32)
    rope_m = _rope_tables(jnp.zeros((1,), jnp.int32), iota(META_PAD))
    pm = _project(meta[None], rope_m, w, tile=META_PAD, emit_q=False, emit_kv=True,
                  emit_gates=False, emit_kpe2=False)

    rope_p = _rope_tables(N_META + ROW_TILE * iota(S // ROW_TILE), iota(ROW_TILE))
    pp = _project(x_prompt, rope_p, w, tile=ROW_TILE, emit_q="transposed", emit_kv=True,
                  emit_gates=True, emit_kpe2=False, lead_rows=N_META)
    o_p = _flash(pp["q"], pp["k"], pp["vt"], pm["k"], pm["vt"])
    y_prompt = _merge_ffn(x_prompt, o_p, pp["u"], pp["gates"], (pm["u"][0],), w,
                          tile=ROW_TILE, n_seq=1, prompt=True)

    rs = Bd * T
    rope_s = _rope_tables(jnp.full((Bd,), past, jnp.int32), iota(T))
    xs = x_sample.reshape(1, rs, D_MODEL)
    ps = _project(xs, rope_s, w, tile=ROW_TILE, emit_q=True, emit_kv=False,
                  emit_gates=True, emit_kpe2=True)
    cache_pe2 = jnp.concatenate([cache_k_rope[0]] * (LANES // QK_ROPE), axis=-1).astype(BF16)
    o_s = _decode(ps["q"][0], ps["c"][0], ps["kpe2"][0], cache_kv_latent[0], cache_pe2,
                  w["w_ukt"], w["w_uv3"], seq=T)
    hist_s = jnp.concatenate([jnp.zeros((Bd, HIST_ROWS - POOL_HIST, D_POOL), F32), cache_pool[0]], axis=1)
    y_sample = _merge_ffn(xs, o_s[None], ps["u"], ps["gates"], (hist_s,), w,
                          tile=ROW_TILE, n_seq=ROW_TILE // T, prompt=False).reshape(Bd, T, D_MODEL)

    def with_meta(m, f):
        return _fill_lead_rows(f, m[0, :N_META])[None]

    c_s = ps["c"].reshape(Bd, T, KV_LORA)
    pe_s = ps["kpe"].reshape(Bd, T, QK_ROPE)
    u_s = ps["u"].reshape(Bd, T, D_POOL)
    return (y_prompt, y_sample,
            with_meta(pm["c"], pp["c"]), with_meta(pm["kpe"], pp["kpe"]), pp["u"][:, -POOL_HIST:][None],
            c_s[None], pe_s[None], u_s[:, -POOL_HIST:][None])
```

```python
import functools
import math

import jax
import jax.numpy as jnp
from jax import lax
from jax.experimental import pallas as pl
from jax.experimental.pallas import tpu as pltpu

F32 = jnp.float32
BF16 = jnp.bfloat16

D_MODEL = 1024
N_HEADS = 8
Q_LORA = 384
KV_LORA = 256
QK_NOPE = 128
QK_ROPE = 64
V_HEAD = 128
ATTN_WIDTH = N_HEADS * V_HEAD
POOL_WINDOWS = (2, 4, 8, 16)
POOL_GROUP = 128
D_POOL = len(POOL_WINDOWS) * POOL_GROUP
POOL_HIST = max(POOL_WINDOWS) - 1
HIST_ROWS = POOL_HIST + 1
D_FF = 4 * D_MODEL
N_META = 16
CHUNK = 64
ROPE_BASE = 10000.0
EPS = 1e-6
SM_SCALE = (QK_NOPE + QK_ROPE) ** -0.5
LOG2E = 1.4426950408889634
NEG_INF = -1e30

LANES = 128
HEAD_W = QK_NOPE + LANES
V_ROWS = V_HEAD + 16
ROW_TILE = 512
META_PAD = 128
Q_STREAMS = 4
FF_CHUNK = 1024
VMEM_LIMIT = 56 * 1024 * 1024


def _nt(a, b):
    return lax.dot_general(a, b, (((1,), (1,)), ((), ())), preferred_element_type=F32)


def _mm(a, b):
    return jnp.dot(a, b, preferred_element_type=F32)


def _rms(x, g):
    return x * lax.rsqrt(jnp.mean(x * x, axis=-1, keepdims=True) + EPS) * g


def _const_spec(shape):
    nd = len(shape)
    return pl.BlockSpec(shape, lambda *_: (0,) * nd, pipeline_mode=pl.Buffered(1))


def _project_body(x_ref, cos_ref, sin_ref, *rest, emit_q, emit_kv, emit_gates, emit_kpe2):
    rest = list(rest)
    cos_t_ref, sin_t_ref = (rest.pop(0), rest.pop(0)) if emit_q == "transposed" else (None, None)
    gmix_ref, gq_ref, gkv_ref, win_ref, wqup_ref, wuk_ref, wuvt_ref = rest[:7]
    outs = rest[7:]
    kvr_w = KV_LORA + 2 * LANES
    c0, c1, c2 = Q_LORA, Q_LORA + kvr_w, Q_LORA + kvr_w + D_POOL
    wql_ref, wkvr_ref = win_ref.at[:, :c0], win_ref.at[:, c0:c1]
    wu_ref, wg_ref = win_ref.at[:, c1:c2], win_ref.at[:, c2:]
    q_ref = outs.pop(0) if emit_q else None
    k_ref, vt_ref = (outs.pop(0), outs.pop(0)) if emit_kv else (None, None)
    c_ref, kpe_ref, u_ref = outs.pop(0), outs.pop(0), outs.pop(0)
    if len(c_ref.shape) == 3:
        c_ref, kpe_ref = c_ref.at[0], kpe_ref.at[0]
    g_ref = outs.pop(0) if emit_gates else None
    kpe2_ref = outs.pop(0) if emit_kpe2 else None

    rows = x_ref.shape[0]
    xn = _rms(x_ref[...], gmix_ref[...]).astype(BF16)
    cos = cos_ref[...]
    sin = sin_ref[...]
    lane = lax.broadcasted_iota(jnp.int32, (rows, LANES), 1)
    half_masks = (lane < QK_ROPE, lane >= QK_ROPE)

    q_lat = _mm(xn, wql_ref[...]) if emit_q else None

    kvr = _mm(xn, wkvr_ref[...])
    c = _rms(kvr[:, :KV_LORA], gkv_ref[...])
    c_ref[...] = c
    kpe2 = kvr[:, KV_LORA:KV_LORA + LANES] * cos + kvr[:, KV_LORA + LANES:] * sin
    kpe_ref[...] = kpe2[:, :QK_ROPE]
    if emit_kpe2:
        kpe2_ref[...] = kpe2.astype(BF16)
    cb = c.astype(BF16)

    nope_w = N_HEADS * QK_NOPE
    rope_w = N_HEADS * QK_ROPE
    if emit_q == "transposed":
        qn = _rms(q_lat, gq_ref[...]).astype(BF16)
        qall = _nt(wqup_ref[...], qn) * (SM_SCALE * LOG2E)
        half = QK_ROPE // 2
        c_t = cos_t_ref[...]
        s_t = sin_t_ref[...]
        zeros = jnp.zeros((QK_ROPE, rows), BF16)
        for h in range(N_HEADS):
            r1 = qall[nope_w + h * QK_ROPE:nope_w + h * QK_ROPE + half]
            r2 = qall[nope_w + h * QK_ROPE + half:nope_w + (h + 1) * QK_ROPE]
            r = jnp.concatenate([r1 * c_t - r2 * s_t, r1 * s_t + r2 * c_t], axis=0).astype(BF16)
            own = h * HEAD_W + QK_NOPE + (h % 2) * QK_ROPE
            other = h * HEAD_W + QK_NOPE + (1 - h % 2) * QK_ROPE
            q_ref[h * HEAD_W:h * HEAD_W + QK_NOPE, :] = qall[h * QK_NOPE:(h + 1) * QK_NOPE].astype(BF16)
            q_ref[own:own + QK_ROPE, :] = r
            q_ref[other:other + QK_ROPE, :] = zeros
    elif emit_q:
        qn = _rms(q_lat, gq_ref[...]).astype(BF16)
        qall = _mm(qn, wqup_ref[...]) * (SM_SCALE * LOG2E)
        for h in range(N_HEADS):
            j = h // 2
            r = qall[:, nope_w + j * LANES:nope_w + (j + 1) * LANES] * cos
            r = r + qall[:, nope_w + rope_w + j * LANES:nope_w + rope_w + (j + 1) * LANES] * sin
            q_ref[:, h * HEAD_W:h * HEAD_W + QK_NOPE] = qall[:, h * QK_NOPE:(h + 1) * QK_NOPE].astype(BF16)
            q_ref[:, h * HEAD_W + QK_NOPE:(h + 1) * HEAD_W] = jnp.where(half_masks[h % 2], r, 0.0).astype(BF16)

    if emit_kv:
        k_nope = _mm(cb, wuk_ref[...])
        for h in range(N_HEADS):
            k_ref[:, h * HEAD_W:h * HEAD_W + QK_NOPE] = k_nope[:, h * QK_NOPE:(h + 1) * QK_NOPE].astype(BF16)
            k_ref[:, h * HEAD_W + QK_NOPE:(h + 1) * HEAD_W] = jnp.where(half_masks[h % 2], kpe2, 0.0).astype(BF16)
        vt = _nt(wuvt_ref[...], cb)
        extra = V_ROWS - V_HEAD
        ones_rows = jnp.where(lax.broadcasted_iota(jnp.int32, (extra, rows), 0) == 0, 1.0, 0.0).astype(BF16)
        for h in range(N_HEADS):
            vt_ref[h * V_ROWS:h * V_ROWS + V_HEAD, :] = vt[h * V_HEAD:(h + 1) * V_HEAD].astype(BF16)
            vt_ref[h * V_ROWS + V_HEAD:(h + 1) * V_ROWS, :] = ones_rows

    if emit_gates:
        logits = _mm(xn, wg_ref[...])
        g_ref[...] = (1.0 / (1.0 + jnp.exp(-logits))).astype(BF16)

    u_ref[...] = _mm(xn, wu_ref[...])


def _project(x, rope, w, *, tile, emit_q, emit_kv, emit_gates, emit_kpe2, lead_rows=0):
    cos, sin, cos_t, sin_t = rope
    G, R, _ = x.shape
    nt = R // tile
    assert nt * tile == R
    row = lambda width: pl.BlockSpec((None, tile, width), lambda g, t: (g, t, 0))
    tab = pl.BlockSpec((tile, LANES), lambda g, t: (t, 0))
    q_t = emit_q == "transposed"
    weights = (w["g_mix"], w["g_q"], w["g_kv"], w["w_in_all"],
               w["w_qup_t"] if q_t else w["w_qup"], w["w_uk"], w["w_uvt"])
    tables = [cos, sin]
    in_specs = [row(D_MODEL), tab, tab]
    if q_t:
        tables += [cos_t, sin_t]
        in_specs += [pl.BlockSpec((QK_ROPE // 2, tile), lambda g, t: (0, t))] * 2
    in_specs += [_const_spec(a.shape) for a in weights]
    out_shape, out_specs = [], []

    def add(shape, spec, dtype):
        out_shape.append(jax.ShapeDtypeStruct(shape, dtype))
        out_specs.append(spec)

    if q_t:
        add((G, N_HEADS * HEAD_W, R), pl.BlockSpec((None, N_HEADS * HEAD_W, tile), lambda g, t: (g, 0, t)), BF16)
    elif emit_q:
        add((G, R, N_HEADS * HEAD_W), row(N_HEADS * HEAD_W), BF16)
    if emit_kv:
        add((G, nt, tile, N_HEADS * HEAD_W),
            pl.BlockSpec((None, None, tile, N_HEADS * HEAD_W), lambda g, t: (g, t, 0, 0)), BF16)
        add((G, nt, N_HEADS * V_ROWS, tile),
            pl.BlockSpec((None, None, N_HEADS * V_ROWS, tile), lambda g, t: (g, t, 0, 0)), BF16)
    cache = row if lead_rows == 0 else (lambda width: pl.BlockSpec(
        (pl.Element(1), pl.Element(tile), pl.Element(width)),
        lambda g, t: (g, pl.multiple_of(lead_rows + t * tile, math.gcd(lead_rows, tile)), 0)))
    add((G, lead_rows + R, KV_LORA), cache(KV_LORA), F32)
    add((G, lead_rows + R, QK_ROPE), cache(QK_ROPE), F32)
    add((G, R, D_POOL), row(D_POOL), F32)
    if emit_gates:
        add((G, R, 2 * D_MODEL), row(2 * D_MODEL), BF16)
    if emit_kpe2:
        add((G, R, LANES), row(LANES), BF16)

    body = functools.partial(_project_body, emit_q=emit_q, emit_kv=emit_kv, emit_gates=emit_gates,
                             emit_kpe2=emit_kpe2)
    outs = pl.pallas_call(
        body,
        grid=(G, nt),
        in_specs=in_specs,
        out_specs=out_specs,
        out_shape=out_shape,
        compiler_params=pltpu.CompilerParams(dimension_semantics=("parallel", "parallel"),
                                             vmem_limit_bytes=VMEM_LIMIT),
        name="project",
    )(x, *tables, *weights)
    outs = list(outs)
    res = {}
    if emit_q:
        res["q"] = outs.pop(0)
    if emit_kv:
        res["k"], res["vt"] = outs.pop(0), outs.pop(0)
    res["c"], res["kpe"], res["u"] = outs.pop(0), outs.pop(0), outs.pop(0)
    if emit_gates:
        res["gates"] = outs.pop(0)
    if emit_kpe2:
        res["kpe2"] = outs.pop(0)
    return res


def _fill_lead_body(big_ref, rows_ref, out_ref):
    del big_ref
    out_ref[...] = rows_ref[...]


def _fill_lead_rows(big, rows):
    G, _, W = big.shape
    n = rows.shape[0]
    return pl.pallas_call(
        _fill_lead_body,
        grid=(G,),
        in_specs=[pl.BlockSpec(memory_space=pl.ANY), pl.BlockSpec((n, W), lambda g: (0, 0))],
        out_specs=pl.BlockSpec((None, n, W), lambda g: (g, 0, 0)),
        out_shape=jax.ShapeDtypeStruct(big.shape, big.dtype),
        input_output_aliases={0: 0},
        name="fill_lead_rows",
    )(big, rows)


def _flash_body(qt_ref, k_ref, vt_ref, km_ref, vtm_ref, o_ref, m_ref, acc_ref, s_ref, cm_ref, *, streams):
    tk = k_ref.shape[1]
    base = pl.program_id(2) * streams

    def produce(buf, kb, ss, slot=None):
        for s in ss:
            dst = s if slot is None else slot
            s_t = _mm(k_ref[kb], qt_ref[:, s * tk:(s + 1) * tk])
            s_ref[buf, dst] = s_t
            cm_ref[buf, dst] = jnp.max(s_t, axis=0, keepdims=True)

    def own_columns(buf, s):
        diag = (lax.broadcasted_iota(jnp.int32, (LANES, LANES), 0) // CHUNK
                <= lax.broadcasted_iota(jnp.int32, (LANES, LANES), 1) // CHUNK)
        cols = []
        for j in range(tk // LANES):
            tiles = [s_ref[buf, s, i * LANES:(i + 1) * LANES, j * LANES:(j + 1) * LANES] for i in range(j + 1)]
            tiles[j] = jnp.where(diag, tiles[j], NEG_INF)
            cols.append(jnp.concatenate(tiles, axis=0))
        return cols

    def consume(buf, s, kb, own=False, slot=None):
        slot = s if slot is None else slot
        m_old = m_ref[s]
        if own:
            cols = own_columns(buf, slot)
            m_blk = jnp.concatenate([jnp.max(c, axis=0, keepdims=True) for c in cols], axis=1)
            m_new = jnp.maximum(m_old, m_blk)
            p_cols = []
            for j, c in enumerate(cols):
                p_j = jnp.exp2(c - m_new[:, j * LANES:(j + 1) * LANES]).astype(BF16)
                if c.shape[0] < tk:
                    p_j = jnp.concatenate([p_j, jnp.zeros((tk - c.shape[0], LANES), BF16)], axis=0)
                p_cols.append(p_j)
            p = jnp.concatenate(p_cols, axis=1)
        else:
            m_new = jnp.maximum(m_old, cm_ref[buf, slot])
            p = jnp.exp2(s_ref[buf, slot] - m_new).astype(BF16)
        acc_ref[s] = jnp.exp2(m_old - m_new) * acc_ref[s] + _mm(vt_ref[kb], p)
        m_ref[s] = m_new

    everyone = range(streams)

    s_m = _mm(km_ref[...], qt_ref[...])
    produce(0, 0, everyone)
    s_m = jnp.where(lax.broadcasted_iota(jnp.int32, s_m.shape, 0) < N_META, s_m, NEG_INF)
    m_0 = jnp.max(s_m, axis=0, keepdims=True)
    acc_0 = _mm(vtm_ref[...], jnp.exp2(s_m - m_0).astype(BF16))
    for s in everyone:
        m_ref[s] = m_0[:, s * tk:(s + 1) * tk]
        acc_ref[s] = acc_0[:, s * tk:(s + 1) * tk]

    def step(rd, wr, kb):
        lead = 0
        for i in range(streams + lead):
            if i < streams:
                produce(wr, kb + 1, [i])
            if i >= lead:
                consume(rd, i - lead, kb)

    per_trip = 4 if streams % 4 == 0 else 2

    def trip(j, carry):
        for i in range(per_trip):
            step(i % 2, 1 - i % 2, per_trip * j + i)
        return carry

    lax.fori_loop(0, base // per_trip, trip, 0)

    def finish(s):
        acc = acc_ref[s]
        o_ref[s * tk:(s + 1) * tk, :] = (acc[:V_HEAD] / acc[V_HEAD:V_HEAD + 1]).T.astype(BF16)

    for s in range(1, streams):
        produce(1, base + s, [s])
        consume(0, s - 1, base, own=(s == 1))
    consume(0, streams - 1, base)
    finish(0)
    left = [(s, base + j) for s in range(2, streams) for j in range(1, s)]
    assert len(left) <= streams
    for slot, (s, kb) in enumerate(left):
        produce(0, kb, [s], slot=slot)
        if slot + 1 < streams:
            consume(1, slot + 1, base + slot + 1, own=True)
    for s in range(len(left) + 1, streams):
        consume(1, s, base + s, own=True)
    finish(1)
    for slot, (s, kb) in enumerate(left):
        consume(0, s, kb, slot=slot)
        if slot + 1 == len(left) or left[slot + 1][0] != s:
            finish(s)


def _flash(q, k, vt, k_meta, vt_meta):
    B, _, S = q.shape
    nt, tile = k.shape[1], k.shape[2]
    streams = Q_STREAMS
    assert nt % streams == 0 and streams % 2 == 0
    return pl.pallas_call(
        functools.partial(_flash_body, streams=streams),
        grid=(B, N_HEADS, nt // streams),
        in_specs=[
            pl.BlockSpec((None, HEAD_W, streams * tile), lambda b, h, i: (b, h, i)),
            pl.BlockSpec((None, nt, tile, HEAD_W), lambda b, h, i: (b, 0, 0, h)),
            pl.BlockSpec((None, nt, V_ROWS, tile), lambda b, h, i: (b, 0, h, 0)),
            pl.BlockSpec((None, None, META_PAD, HEAD_W), lambda b, h, i: (0, 0, 0, h)),
            pl.BlockSpec((None, None, V_ROWS, META_PAD), lambda b, h, i: (0, 0, h, 0)),
        ],
        out_specs=pl.BlockSpec((None, streams * tile, V_HEAD), lambda b, h, i: (b, i, h)),
        out_shape=jax.ShapeDtypeStruct((B, S, ATTN_WIDTH), BF16),
        scratch_shapes=[pltpu.VMEM((streams, 1, tile), F32),
                        pltpu.VMEM((streams, V_ROWS, tile), F32),
                        pltpu.VMEM((2, streams, tile, tile), F32), pltpu.VMEM((2, streams, 1, tile), F32)],
        compiler_params=pltpu.CompilerParams(dimension_semantics=("parallel", "parallel", "arbitrary"),
                                             vmem_limit_bytes=VMEM_LIMIT),
        name="flash",
    )(q, k, vt, k_meta, vt_meta)


def _decode_body(q_ref, cn_ref, pn_ref, cc_ref, pc_ref, wukt_ref, wuv_ref, o_ref):
    q = q_ref[...]
    t = q.shape[0]
    qa = jnp.concatenate(
        [_mm(q[:, h * HEAD_W:h * HEAD_W + QK_NOPE], wukt_ref[h]) for h in range(N_HEADS)], axis=0).astype(BF16)
    qp = jnp.concatenate([q[:, h * HEAD_W + QK_NOPE:(h + 1) * HEAD_W] for h in range(N_HEADS)], axis=0)
    cc = cc_ref[...].astype(BF16)
    cn = cn_ref[...].astype(BF16)
    s_c = _nt(qa, cc) + _nt(qp, pc_ref[...])
    s_n = _nt(qa, cn) + _nt(qp, pn_ref[...])
    m = jnp.maximum(jnp.max(s_c, axis=-1, keepdims=True), jnp.max(s_n, axis=-1, keepdims=True))
    p_c = jnp.exp2(s_c - m)
    p_n = jnp.exp2(s_n - m)
    l = jnp.sum(p_c, axis=-1, keepdims=True) + jnp.sum(p_n, axis=-1, keepdims=True)
    o_lat = ((_mm(p_c.astype(BF16), cc) + _mm(p_n.astype(BF16), cn)) / l).astype(BF16)
    for h in range(N_HEADS):
        o_ref[:, h * V_HEAD:(h + 1) * V_HEAD] = _mm(o_lat[h * t:(h + 1) * t], wuv_ref[h]).astype(BF16)


def _decode(q, c_new, kpe2_new, cache_c, cache_pe2, wukt, wuv, *, seq):
    R = q.shape[0]
    nb, past, _ = cache_c.shape
    assert nb * seq == R
    return pl.pallas_call(
        _decode_body,
        grid=(nb,),
        in_specs=[
            pl.BlockSpec((seq, N_HEADS * HEAD_W), lambda i: (i, 0)),
            pl.BlockSpec((seq, KV_LORA), lambda i: (i, 0)),
            pl.BlockSpec((seq, LANES), lambda i: (i, 0)),
            pl.BlockSpec((None, past, KV_LORA), lambda i: (i, 0, 0)),
            pl.BlockSpec((None, past, LANES), lambda i: (i, 0, 0)),
            _const_spec(wukt.shape),
            _const_spec(wuv.shape),
        ],
        out_specs=pl.BlockSpec((seq, ATTN_WIDTH), lambda i: (i, 0)),
        out_shape=jax.ShapeDtypeStruct((R, ATTN_WIDTH), BF16),
        compiler_params=pltpu.CompilerParams(dimension_semantics=("parallel",),
                                             vmem_limit_bytes=VMEM_LIMIT),
        name="decode",
    )(q, c_new, kpe2_new, cache_c, cache_pe2, wukt, wuv)


def _pool_seq(hist, u):
    ext = jnp.concatenate([hist, u], axis=0)
    outs = []
    for g, w in enumerate(POOL_WINDOWS):
        e = ext[:, g * POOL_GROUP:(g + 1) * POOL_GROUP]
        s = e
        shift = 1
        while shift < w:
            s = s + pltpu.roll(s, shift, axis=0)
            shift *= 2
        outs.append(s[HIST_ROWS:] * (1.0 / w) - e[HIST_ROWS:])
    return outs


def _merge_body(*refs, n_seq, prompt):
    if prompt:
        (x_ref, o_ref, u_ref, hprev_ref, hmeta_ref, g_ref, wab_ref, wpg_ref, ps_ref, wpb_ref, wout_ref,
         gffn_ref, wup_ref, wdown_ref, gfin_ref, y_ref) = refs
        first = pl.program_id(1) == 0
        hists = [jnp.where(first, hmeta_ref[...], hprev_ref[...])]
    else:
        (x_ref, o_ref, u_ref, hist_ref, g_ref, wab_ref, wpg_ref, ps_ref, wpb_ref, wout_ref,
         gffn_ref, wup_ref, wdown_ref, gfin_ref, y_ref) = refs
        hists = [hist_ref[i] for i in range(n_seq)]

    rows = x_ref.shape[0]
    t = rows // n_seq
    halves = [slice(0, rows // 2), slice(rows // 2, rows)]
    a = [_mm(o_ref[hs, :], wab_ref[...]) for hs in halves]
    u = u_ref[...]
    per_seq = [_pool_seq(hists[i], u[i * t:(i + 1) * t]) for i in range(n_seq)]
    mixed = []
    for g in range(len(POOL_WINDOWS)):
        pooled = jnp.concatenate([per_seq[i][g] for i in range(n_seq)], axis=0) if n_seq > 1 else per_seq[0][g]
        mixed.append(_mm(pooled.astype(BF16), wpg_ref[g]))
    pm = (jnp.concatenate(mixed, axis=-1) * ps_ref[...]).astype(BF16)
    p = [_mm(pm[hs], wpb_ref[...]) for hs in halves]
    h = []
    for i, hs in enumerate(halves):
        gates = g_ref[hs, :].astype(F32)
        mix = (gates[:, :D_MODEL] * a[i] + gates[:, D_MODEL:] * p[i]).astype(BF16)
        h.append(x_ref[hs, :] + _mm(mix, wout_ref[...]))
    hn = [_rms(h_i, gffn_ref[...]).astype(BF16) for h_i in h]
    for c in range(D_FF // FF_CHUNK):
        cols = slice(c * FF_CHUNK, (c + 1) * FF_CHUNK)
        up = [jnp.maximum(_mm(hn_i, wup_ref[:, cols]), 0.0) for hn_i in hn]
        h = [h_i + _mm((up_i * up_i).astype(BF16), wdown_ref[cols, :]) for h_i, up_i in zip(h, up)]
    for hs, h_i in zip(halves, h):
        y_ref[hs, :] = _rms(h_i, gfin_ref[...])


def _merge_ffn(x, o, u, gates, hist_args, w, *, tile, n_seq, prompt):
    G, R, _ = x.shape
    nt = R // tile
    row = lambda width: pl.BlockSpec((None, tile, width), lambda g, t: (g, t, 0))
    if prompt:
        (u_meta,) = hist_args
        per_tile = tile // HIST_ROWS
        hist_specs = [
            pl.BlockSpec((None, HIST_ROWS, D_POOL), lambda g, t: (g, jnp.maximum(t * per_tile - 1, 0), 0)),
            pl.BlockSpec((HIST_ROWS, D_POOL), lambda g, t: (0, 0)),
        ]
        hist_in = [u, u_meta]
    else:
        (hist,) = hist_args
        hist_specs = [pl.BlockSpec((n_seq, HIST_ROWS, D_POOL), lambda g, t: (t, 0, 0))]
        hist_in = [hist]
    weights = (w["w_attn_br"], w["w_pool_grp"], w["pool_scale"], w["w_pool_br"], w["w_out"],
               w["g_ffn"], w["w_up"], w["w_down"], w["g_final"])
    in_specs = ([row(D_MODEL), row(ATTN_WIDTH), row(D_POOL)] + hist_specs + [row(2 * D_MODEL)]
                + [_const_spec(a.shape) for a in weights])
    return pl.pallas_call(
        functools.partial(_merge_body, n_seq=n_seq, prompt=prompt),
        grid=(G, nt),
        in_specs=in_specs,
        out_specs=row(D_MODEL),
        out_shape=jax.ShapeDtypeStruct((G, R, D_MODEL), F32),
        compiler_params=pltpu.CompilerParams(dimension_semantics=("parallel", "parallel"),
                                             vmem_limit_bytes=VMEM_LIMIT),
        name="merge_ffn",
    )(x, o, u, *hist_in, gates, *weights)


def _rope_tables(base_pos, off_pos):
    half = QK_ROPE // 2
    inv_freq = jnp.exp(-math.log(ROPE_BASE) * jnp.arange(half, dtype=jnp.float32) / half)
    reps = LANES // half
    freq = jnp.tile(inv_freq, reps)
    sign = jnp.tile(jnp.concatenate([-jnp.ones(half, F32), jnp.ones(half, F32)]), reps // 2)
    a = base_pos.astype(jnp.float32)[:, None, None] * freq
    b = off_pos.astype(jnp.float32)[None, :, None] * freq
    cos = (jnp.cos(a) * jnp.cos(b) - jnp.sin(a) * jnp.sin(b)).reshape(-1, LANES)
    sin = ((jnp.sin(a) * jnp.cos(b) + jnp.cos(a) * jnp.sin(b)) * sign).reshape(-1, LANES)
    a_t = inv_freq[:, None, None] * base_pos.astype(jnp.float32)[None, :, None]
    b_t = inv_freq[:, None, None] * off_pos.astype(jnp.float32)[None, None, :]
    cos_t = (jnp.cos(a_t) * jnp.cos(b_t) - jnp.sin(a_t) * jnp.sin(b_t)).reshape(half, -1)
    sin_t = (jnp.sin(a_t) * jnp.cos(b_t) + jnp.cos(a_t) * jnp.sin(b_t)).reshape(half, -1)
    return cos, sin, cos_t, sin_t


def _prep_weights(l, w_in, g_norm_mix, g_q, g_kv, w_q_up, w_uk, w_uv, w_attn_br, w_pool_grp,
                  pool_scale, w_pool_br, w_out, g_norm_ffn, w_up, w_down, g_final):
    half = QK_ROPE // 2
    i0, i1, i2, i3 = Q_LORA, Q_LORA + KV_LORA, Q_LORA + KV_LORA + QK_ROPE, Q_LORA + KV_LORA + QK_ROPE + D_POOL
    wi = w_in[l]
    w_kr = wi[:, i1:i2]
    w_kr_sw = jnp.concatenate([w_kr[:, half:], w_kr[:, :half]], axis=-1)
    wq = w_q_up[l].reshape(Q_LORA, N_HEADS, QK_NOPE + QK_ROPE)
    wq_r1, wq_r2 = wq[:, :, QK_NOPE:QK_NOPE + half], wq[:, :, QK_NOPE + half:]
    row = lambda v: v.reshape(1, -1).astype(F32)
    w_qup = jnp.concatenate([
        wq[:, :, :QK_NOPE].reshape(Q_LORA, -1),
        jnp.concatenate([wq_r1, wq_r2], axis=-1).reshape(Q_LORA, -1),
        jnp.concatenate([wq_r2, wq_r1], axis=-1).reshape(Q_LORA, -1)], axis=-1).astype(BF16)
    return {
        "g_mix": row(g_norm_mix[l]), "g_q": row(g_q[l]), "g_kv": row(g_kv[l]),
        "w_in_all": jnp.concatenate([wi[:, :i0], wi[:, i0:i1], w_kr, w_kr, w_kr_sw, w_kr_sw,
                                     wi[:, i2:i3], wi[:, i3:]], axis=-1).astype(BF16),
        "w_qup": w_qup,
        "w_qup_t": w_qup[:, :N_HEADS * (QK_NOPE + QK_ROPE)].T,
        "w_uk": w_uk[l].reshape(KV_LORA, N_HEADS * QK_NOPE).astype(BF16),
        "w_uvt": w_uv[l].reshape(KV_LORA, N_HEADS * V_HEAD).T.astype(BF16),
        "w_ukt": jnp.transpose(w_uk[l], (1, 2, 0)).astype(BF16),
        "w_uv3": jnp.transpose(w_uv[l], (1, 0, 2)).astype(BF16),
        "w_attn_br": w_attn_br[l].astype(BF16),
        "w_pool_grp": w_pool_grp[l].astype(BF16),
        "pool_scale": row(pool_scale[l]),
        "w_pool_br": w_pool_br[l].astype(BF16),
        "w_out": w_out[l].astype(BF16),
        "g_ffn": row(g_norm_ffn[l]),
        "w_up": w_up[l].astype(BF16),
        "w_down": w_down[l].astype(BF16),
        "g_final": row(g_final),
    }


def kernel(x_prompt, x_sample, cache_kv_latent, cache_k_rope, cache_pool, meta_tokens, w_in, g_norm_mix, g_q, g_kv, w_q_up, w_uk, w_uv, w_attn_br, w_pool_grp, pool_scale, w_pool_br, w_out, g_norm_ffn, w_up, w_down, g_final):
    B, S, _ = x_prompt.shape
    Bd, T, _ = x_sample.shape
    past = cache_kv_latent.shape[2]
    assert w_in.shape[0] == 1 and S % ROW_TILE == 0 and (Bd * T) % ROW_TILE == 0 and ROW_TILE % T == 0
    assert N_META >= max(POOL_WINDOWS) and POOL_HIST + 1 >= max(POOL_WINDOWS) and T >= POOL_HIST
    w = _prep_weights(0, w_in, g_norm_mix, g_q, g_kv, w_q_up, w_uk, w_uv, w_attn_br, w_pool_grp,
                      pool_scale, w_pool_br, w_out, g_norm_ffn, w_up, w_down, g_final)

    meta = jnp.concatenate([meta_tokens.astype(F32), jnp.zeros((META_PAD - N_META, D_MODEL), F32)], axis=0)
    iota = functools.partial(jnp.arange, dtype=jnp.int32)
    rope_m = _rope_tables(jnp.zeros((1,), jnp.int32), iota(META_PAD))
    pm = _project(meta[None], rope_m, w, tile=META_PAD, emit_q=False, emit_kv=True,
                  emit_gates=False, emit_kpe2=False)

    rope_p = _rope_tables(N_META + ROW_TILE * iota(S // ROW_TILE), iota(ROW_TILE))
    pp = _project(x_prompt, rope_p, w, tile=ROW_TILE, emit_q="transposed", emit_kv=True,
                  emit_gates=True, emit_kpe2=False, lead_rows=N_META)
    o_p = _flash(pp["q"], pp["k"], pp["vt"], pm["k"], pm["vt"])
    y_prompt = _merge_ffn(x_prompt, o_p, pp["u"], pp["gates"], (pm["u"][0],), w,
                          tile=ROW_TILE, n_seq=1, prompt=True)

    rs = Bd * T
    rope_s = _rope_tables(jnp.full((Bd,), past, jnp.int32), iota(T))
    xs = x_sample.reshape(1, rs, D_MODEL)
    ps = _project(xs, rope_s, w, tile=ROW_TILE, emit_q=True, emit_kv=False,
                  emit_gates=True, emit_kpe2=True)
    cache_pe2 = jnp.concatenate([cache_k_rope[0]] * (LANES // QK_ROPE), axis=-1).astype(BF16)
    o_s = _decode(ps["q"][0], ps["c"][0], ps["kpe2"][0], cache_kv_latent[0], cache_pe2,
                  w["w_ukt"], w["w_uv3"], seq=T)
    hist_s = jnp.concatenate([jnp.zeros((Bd, HIST_ROWS - POOL_HIST, D_POOL), F32), cache_pool[0]], axis=1)
    y_sample = _merge_ffn(xs, o_s[None], ps["u"], ps["gates"], (hist_s,), w,
                          tile=ROW_TILE, n_seq=ROW_TILE // T, prompt=False).reshape(Bd, T, D_MODEL)

    def with_meta(m, f):
        return _fill_lead_rows(f, m[0, :N_META])[None]

    c_s = ps["c"].reshape(Bd, T, KV_LORA)
    pe_s = ps["kpe"].reshape(Bd, T, QK_ROPE)
    u_s = ps["u"].reshape(Bd, T, D_POOL)
    return (y_prompt, y_sample,
            with_meta(pm["c"], pp["c"]), with_meta(pm["kpe"], pp["kpe"]), pp["u"][:, -POOL_HIST:][None],
            c_s[None], pe_s[None], u_s[:, -POOL_HIST:][None])
```

```python
import functools
import math

import jax
import jax.numpy as jnp
from jax import lax
from jax.experimental import pallas as pl
from jax.experimental.pallas import tpu as pltpu

F32 = jnp.float32
BF16 = jnp.bfloat16

D_MODEL = 1024
N_HEADS = 8
Q_LORA = 384
KV_LORA = 256
QK_NOPE = 128
QK_ROPE = 64
V_HEAD = 128
ATTN_WIDTH = N_HEADS * V_HEAD
POOL_WINDOWS = (2, 4, 8, 16)
POOL_GROUP = 128
D_POOL = len(POOL_WINDOWS) * POOL_GROUP
POOL_HIST = max(POOL_WINDOWS) - 1
HIST_ROWS = POOL_HIST + 1
D_FF = 4 * D_MODEL
N_META = 16
CHUNK = 64
ROPE_BASE = 10000.0
EPS = 1e-6
SM_SCALE = (QK_NOPE + QK_ROPE) ** -0.5
LOG2E = 1.4426950408889634
NEG_INF = -1e30

LANES = 128
HEAD_W = QK_NOPE + LANES
V_ROWS = V_HEAD + 16
ROW_TILE = 512
META_PAD = 128
Q_STREAMS = 4
FF_CHUNK = 1024
VMEM_LIMIT = 56 * 1024 * 1024


def _nt(a, b):
    return lax.dot_general(a, b, (((1,), (1,)), ((), ())), preferred_element_type=F32)


def _mm(a, b):
    return jnp.dot(a, b, preferred_element_type=F32)


def _rms(x, g):
    return x * lax.rsqrt(jnp.mean(x * x, axis=-1, keepdims=True) + EPS) * g


def _const_spec(shape):
    nd = len(shape)
    return pl.BlockSpec(shape, lambda *_: (0,) * nd, pipeline_mode=pl.Buffered(1))


def _project_body(x_ref, cos_ref, sin_ref, *rest, emit_q, emit_kv, emit_gates, emit_kpe2):
    rest = list(rest)
    cos_t_ref, sin_t_ref = (rest.pop(0), rest.pop(0)) if emit_q == "transposed" else (None, None)
    gmix_ref, gq_ref, gkv_ref, wql_ref, wkvr_ref, wu_ref, wg_ref, wqup_ref, wuk_ref, wuvt_ref = rest[:10]
    outs = rest[10:]
    q_ref = outs.pop(0) if emit_q else None
    k_ref, vt_ref = (outs.pop(0), outs.pop(0)) if emit_kv else (None, None)
    c_ref, kpe_ref, u_ref = outs.pop(0), outs.pop(0), outs.pop(0)
    if len(c_ref.shape) == 3:
        c_ref, kpe_ref = c_ref.at[0], kpe_ref.at[0]
    g_ref = outs.pop(0) if emit_gates else None
    kpe2_ref = outs.pop(0) if emit_kpe2 else None

    rows = x_ref.shape[0]
    xn = _rms(x_ref[...], gmix_ref[...]).astype(BF16)
    cos = cos_ref[...]
    sin = sin_ref[...]
    lane = lax.broadcasted_iota(jnp.int32, (rows, LANES), 1)
    half_masks = (lane < QK_ROPE, lane >= QK_ROPE)

    q_lat = _mm(xn, wql_ref[...]) if emit_q else None

    kvr = _mm(xn, wkvr_ref[...])
    c = _rms(kvr[:, :KV_LORA], gkv_ref[...])
    c_ref[...] = c
    kpe2 = kvr[:, KV_LORA:KV_LORA + LANES] * cos + kvr[:, KV_LORA + LANES:] * sin
    kpe_ref[...] = kpe2[:, :QK_ROPE]
    if emit_kpe2:
        kpe2_ref[...] = kpe2.astype(BF16)
    cb = c.astype(BF16)

    nope_w = N_HEADS * QK_NOPE
    rope_w = N_HEADS * QK_ROPE
    if emit_q == "transposed":
        qn = _rms(q_lat, gq_ref[...]).astype(BF16)
        qall = _nt(wqup_ref[...], qn) * (SM_SCALE * LOG2E)
        half = QK_ROPE // 2
        c_t = cos_t_ref[...]
        s_t = sin_t_ref[...]
        zeros = jnp.zeros((QK_ROPE, rows), BF16)
        for h in range(N_HEADS):
            r1 = qall[nope_w + h * QK_ROPE:nope_w + h * QK_ROPE + half]
            r2 = qall[nope_w + h * QK_ROPE + half:nope_w + (h + 1) * QK_ROPE]
            r = jnp.concatenate([r1 * c_t - r2 * s_t, r1 * s_t + r2 * c_t], axis=0).astype(BF16)
            own = h * HEAD_W + QK_NOPE + (h % 2) * QK_ROPE
            other = h * HEAD_W + QK_NOPE + (1 - h % 2) * QK_ROPE
            q_ref[h * HEAD_W:h * HEAD_W + QK_NOPE, :] = qall[h * QK_NOPE:(h + 1) * QK_NOPE].astype(BF16)
            q_ref[own:own + QK_ROPE, :] = r
            q_ref[other:other + QK_ROPE, :] = zeros
    elif emit_q:
        qn = _rms(q_lat, gq_ref[...]).astype(BF16)
        qall = _mm(qn, wqup_ref[...]) * (SM_SCALE * LOG2E)
        for h in range(N_HEADS):
            j = h // 2
            r = qall[:, nope_w + j * LANES:nope_w + (j + 1) * LANES] * cos
            r = r + qall[:, nope_w + rope_w + j * LANES:nope_w + rope_w + (j + 1) * LANES] * sin
            q_ref[:, h * HEAD_W:h * HEAD_W + QK_NOPE] = qall[:, h * QK_NOPE:(h + 1) * QK_NOPE].astype(BF16)
            q_ref[:, h * HEAD_W + QK_NOPE:(h + 1) * HEAD_W] = jnp.where(half_masks[h % 2], r, 0.0).astype(BF16)

    if emit_kv:
        k_nope = _mm(cb, wuk_ref[...])
        for h in range(N_HEADS):
            k_ref[:, h * HEAD_W:h * HEAD_W + QK_NOPE] = k_nope[:, h * QK_NOPE:(h + 1) * QK_NOPE].astype(BF16)
            k_ref[:, h * HEAD_W + QK_NOPE:(h + 1) * HEAD_W] = jnp.where(half_masks[h % 2], kpe2, 0.0).astype(BF16)
        vt = _nt(wuvt_ref[...], cb)
        extra = V_ROWS - V_HEAD
        ones_rows = jnp.where(lax.broadcasted_iota(jnp.int32, (extra, rows), 0) == 0, 1.0, 0.0).astype(BF16)
        for h in range(N_HEADS):
            vt_ref[h * V_ROWS:h * V_ROWS + V_HEAD, :] = vt[h * V_HEAD:(h + 1) * V_HEAD].astype(BF16)
            vt_ref[h * V_ROWS + V_HEAD:(h + 1) * V_ROWS, :] = ones_rows

    if emit_gates:
        logits = _mm(xn, wg_ref[...])
        g_ref[...] = (1.0 / (1.0 + jnp.exp(-logits))).astype(BF16)

    u_ref[...] = _mm(xn, wu_ref[...])


def _project(x, rope, w, *, tile, emit_q, emit_kv, emit_gates, emit_kpe2, lead_rows=0):
    cos, sin, cos_t, sin_t = rope
    G, R, _ = x.shape
    nt = R // tile
    assert nt * tile == R
    row = lambda width: pl.BlockSpec((None, tile, width), lambda g, t: (g, t, 0))
    tab = pl.BlockSpec((tile, LANES), lambda g, t: (t, 0))
    q_t = emit_q == "transposed"
    weights = (w["g_mix"], w["g_q"], w["g_kv"], w["w_qlat"], w["w_kvr"], w["w_u"], w["w_g"],
               w["w_qup_t"] if q_t else w["w_qup"], w["w_uk"], w["w_uvt"])
    tables = [cos, sin]
    in_specs = [row(D_MODEL), tab, tab]
    if q_t:
        tables += [cos_t, sin_t]
        in_specs += [pl.BlockSpec((QK_ROPE // 2, tile), lambda g, t: (0, t))] * 2
    in_specs += [_const_spec(a.shape) for a in weights]
    out_shape, out_specs = [], []

    def add(shape, spec, dtype):
        out_shape.append(jax.ShapeDtypeStruct(shape, dtype))
        out_specs.append(spec)

    if q_t:
        add((G, N_HEADS * HEAD_W, R), pl.BlockSpec((None, N_HEADS * HEAD_W, tile), lambda g, t: (g, 0, t)), BF16)
    elif emit_q:
        add((G, R, N_HEADS * HEAD_W), row(N_HEADS * HEAD_W), BF16)
    if emit_kv:
        add((G, nt, tile, N_HEADS * HEAD_W),
            pl.BlockSpec((None, None, tile, N_HEADS * HEAD_W), lambda g, t: (g, t, 0, 0)), BF16)
        add((G, nt, N_HEADS * V_ROWS, tile),
            pl.BlockSpec((None, None, N_HEADS * V_ROWS, tile), lambda g, t: (g, t, 0, 0)), BF16)
    cache = row if lead_rows == 0 else (lambda width: pl.BlockSpec(
        (pl.Element(1), pl.Element(tile), pl.Element(width)),
        lambda g, t: (g, pl.multiple_of(lead_rows + t * tile, math.gcd(lead_rows, tile)), 0)))
    add((G, lead_rows + R, KV_LORA), cache(KV_LORA), F32)
    add((G, lead_rows + R, QK_ROPE), cache(QK_ROPE), F32)
    add((G, R, D_POOL), row(D_POOL), F32)
    if emit_gates:
        add((G, R, 2 * D_MODEL), row(2 * D_MODEL), BF16)
    if emit_kpe2:
        add((G, R, LANES), row(LANES), BF16)

    body = functools.partial(_project_body, emit_q=emit_q, emit_kv=emit_kv, emit_gates=emit_gates,
                             emit_kpe2=emit_kpe2)
    outs = pl.pallas_call(
        body,
        grid=(G, nt),
        in_specs=in_specs,
        out_specs=out_specs,
        out_shape=out_shape,
        compiler_params=pltpu.CompilerParams(dimension_semantics=("parallel", "parallel"),
                                             vmem_limit_bytes=VMEM_LIMIT),
        name="project",
    )(x, *tables, *weights)
    outs = list(outs)
    res = {}
    if emit_q:
        res["q"] = outs.pop(0)
    if emit_kv:
        res["k"], res["vt"] = outs.pop(0), outs.pop(0)
    res["c"], res["kpe"], res["u"] = outs.pop(0), outs.pop(0), outs.pop(0)
    if emit_gates:
        res["gates"] = outs.pop(0)
    if emit_kpe2:
        res["kpe2"] = outs.pop(0)
    return res


def _fill_lead_body(big_ref, rows_ref, out_ref):
    del big_ref
    out_ref[...] = rows_ref[...]


def _fill_lead_rows(big, rows):
    G, _, W = big.shape
    n = rows.shape[0]
    return pl.pallas_call(
        _fill_lead_body,
        grid=(G,),
        in_specs=[pl.BlockSpec(memory_space=pl.ANY), pl.BlockSpec((n, W), lambda g: (0, 0))],
        out_specs=pl.BlockSpec((None, n, W), lambda g: (g, 0, 0)),
        out_shape=jax.ShapeDtypeStruct(big.shape, big.dtype),
        input_output_aliases={0: 0},
        name="fill_lead_rows",
    )(big, rows)


def _flash_body(qt_ref, k_ref, vt_ref, km_ref, vtm_ref, o_ref, m_ref, acc_ref, s_ref, cm_ref, *, streams):
    tk = k_ref.shape[1]
    base = pl.program_id(2) * streams

    def produce(buf, kb, ss, slot=None):
        for s in ss:
            dst = s if slot is None else slot
            s_t = _mm(k_ref[kb], qt_ref[:, s * tk:(s + 1) * tk])
            s_ref[buf, dst] = s_t
            cm_ref[buf, dst] = jnp.max(s_t, axis=0, keepdims=True)

    def own_columns(buf, s):
        diag = (lax.broadcasted_iota(jnp.int32, (LANES, LANES), 0) // CHUNK
                <= lax.broadcasted_iota(jnp.int32, (LANES, LANES), 1) // CHUNK)
        cols = []
        for j in range(tk // LANES):
            tiles = [s_ref[buf, s, i * LANES:(i + 1) * LANES, j * LANES:(j + 1) * LANES] for i in range(j + 1)]
            tiles[j] = jnp.where(diag, tiles[j], NEG_INF)
            cols.append(jnp.concatenate(tiles, axis=0))
        return cols

    def consume(buf, s, kb, own=False, slot=None):
        slot = s if slot is None else slot
        m_old = m_ref[s]
        if own:
            cols = own_columns(buf, slot)
            m_blk = jnp.concatenate([jnp.max(c, axis=0, keepdims=True) for c in cols], axis=1)
            m_new = jnp.maximum(m_old, m_blk)
            p_cols = []
            for j, c in enumerate(cols):
                p_j = jnp.exp2(c - m_new[:, j * LANES:(j + 1) * LANES]).astype(BF16)
                if c.shape[0] < tk:
                    p_j = jnp.concatenate([p_j, jnp.zeros((tk - c.shape[0], LANES), BF16)], axis=0)
                p_cols.append(p_j)
            p = jnp.concatenate(p_cols, axis=1)
        else:
            m_new = jnp.maximum(m_old, cm_ref[buf, slot])
            p = jnp.exp2(s_ref[buf, slot] - m_new).astype(BF16)
        acc_ref[s] = jnp.exp2(m_old - m_new) * acc_ref[s] + _mm(vt_ref[kb], p)
        m_ref[s] = m_new

    everyone = range(streams)

    s_m = _mm(km_ref[...], qt_ref[...])
    produce(0, 0, everyone)
    s_m = jnp.where(lax.broadcasted_iota(jnp.int32, s_m.shape, 0) < N_META, s_m, NEG_INF)
    m_0 = jnp.max(s_m, axis=0, keepdims=True)
    acc_0 = _mm(vtm_ref[...], jnp.exp2(s_m - m_0).astype(BF16))
    for s in everyone:
        m_ref[s] = m_0[:, s * tk:(s + 1) * tk]
        acc_ref[s] = acc_0[:, s * tk:(s + 1) * tk]

    def step(rd, wr, kb):
        lead = 0
        for i in range(streams + lead):
            if i < streams:
                produce(wr, kb + 1, [i])
            if i >= lead:
                consume(rd, i - lead, kb)

    per_trip = 4 if streams % 4 == 0 else 2

    def trip(j, carry):
        for i in range(per_trip):
            step(i % 2, 1 - i % 2, per_trip * j + i)
        return carry

    lax.fori_loop(0, base // per_trip, trip, 0)

    def finish(s):
        acc = acc_ref[s]
        o_ref[s * tk:(s + 1) * tk, :] = (acc[:V_HEAD] / acc[V_HEAD:V_HEAD + 1]).T.astype(BF16)

    for s in range(1, streams):
        produce(1, base + s, [s])
        consume(0, s - 1, base, own=(s == 1))
    consume(0, streams - 1, base)
    finish(0)
    left = [(s, base + j) for s in range(2, streams) for j in range(1, s)]
    assert len(left) <= streams
    for slot, (s, kb) in enumerate(left):
        produce(0, kb, [s], slot=slot)
        if slot + 1 < streams:
            consume(1, slot + 1, base + slot + 1, own=True)
    for s in range(len(left) + 1, streams):
        consume(1, s, base + s, own=True)
    finish(1)
    for slot, (s, kb) in enumerate(left):
        consume(0, s, kb, slot=slot)
        if slot + 1 == len(left) or left[slot + 1][0] != s:
            finish(s)


def _flash(q, k, vt, k_meta, vt_meta):
    B, _, S = q.shape
    nt, tile = k.shape[1], k.shape[2]
    streams = Q_STREAMS
    assert nt % streams == 0 and streams % 2 == 0
    return pl.pallas_call(
        functools.partial(_flash_body, streams=streams),
        grid=(B, N_HEADS, nt // streams),
        in_specs=[
            pl.BlockSpec((None, HEAD_W, streams * tile), lambda b, h, i: (b, h, i)),
            pl.BlockSpec((None, nt, tile, HEAD_W), lambda b, h, i: (b, 0, 0, h)),
            pl.BlockSpec((None, nt, V_ROWS, tile), lambda b, h, i: (b, 0, h, 0)),
            pl.BlockSpec((None, None, META_PAD, HEAD_W), lambda b, h, i: (0, 0, 0, h)),
            pl.BlockSpec((None, None, V_ROWS, META_PAD), lambda b, h, i: (0, 0, h, 0)),
        ],
        out_specs=pl.BlockSpec((None, streams * tile, V_HEAD), lambda b, h, i: (b, i, h)),
        out_shape=jax.ShapeDtypeStruct((B, S, ATTN_WIDTH), BF16),
        scratch_shapes=[pltpu.VMEM((streams, 1, tile), F32),
                        pltpu.VMEM((streams, V_ROWS, tile), F32),
                        pltpu.VMEM((2, streams, tile, tile), F32), pltpu.VMEM((2, streams, 1, tile), F32)],
        compiler_params=pltpu.CompilerParams(dimension_semantics=("parallel", "parallel", "arbitrary"),
                                             vmem_limit_bytes=VMEM_LIMIT),
        name="flash",
    )(q, k, vt, k_meta, vt_meta)


def _decode_body(q_ref, cn_ref, pn_ref, cc_ref, pc_ref, wukt_ref, wuv_ref, o_ref):
    q = q_ref[...]
    t = q.shape[0]
    qa = jnp.concatenate(
        [_mm(q[:, h * HEAD_W:h * HEAD_W + QK_NOPE], wukt_ref[h]) for h in range(N_HEADS)], axis=0).astype(BF16)
    qp = jnp.concatenate([q[:, h * HEAD_W + QK_NOPE:(h + 1) * HEAD_W] for h in range(N_HEADS)], axis=0)
    cc = cc_ref[...].astype(BF16)
    cn = cn_ref[...].astype(BF16)
    pc_t = pc_ref[...].astype(BF16)
    s_c = _nt(qa, cc) + _mm(qp, jnp.concatenate([pc_t] * (LANES // QK_ROPE), axis=0))
    s_n = _nt(qa, cn) + _nt(qp, pn_ref[...])
    m = jnp.maximum(jnp.max(s_c, axis=-1, keepdims=True), jnp.max(s_n, axis=-1, keepdims=True))
    p_c = jnp.exp2(s_c - m)
    p_n = jnp.exp2(s_n - m)
    l = jnp.sum(p_c, axis=-1, keepdims=True) + jnp.sum(p_n, axis=-1, keepdims=True)
    o_lat = ((_mm(p_c.astype(BF16), cc) + _mm(p_n.astype(BF16), cn)) / l).astype(BF16)
    for h in range(N_HEADS):
        o_ref[:, h * V_HEAD:(h + 1) * V_HEAD] = _mm(o_lat[h * t:(h + 1) * t], wuv_ref[h]).astype(BF16)


def _decode(q, c_new, kpe2_new, cache_c, cache_pe_t, wukt, wuv, *, seq):
    R = q.shape[0]
    nb, past, _ = cache_c.shape
    assert nb * seq == R
    return pl.pallas_call(
        _decode_body,
        grid=(nb,),
        in_specs=[
            pl.BlockSpec((seq, N_HEADS * HEAD_W), lambda i: (i, 0)),
            pl.BlockSpec((seq, KV_LORA), lambda i: (i, 0)),
            pl.BlockSpec((seq, LANES), lambda i: (i, 0)),
            pl.BlockSpec((None, past, KV_LORA), lambda i: (i, 0, 0)),
            pl.BlockSpec((None, QK_ROPE, past), lambda i: (i, 0, 0)),
            _const_spec(wukt.shape),
            _const_spec(wuv.shape),
        ],
        out_specs=pl.BlockSpec((seq, ATTN_WIDTH), lambda i: (i, 0)),
        out_shape=jax.ShapeDtypeStruct((R, ATTN_WIDTH), BF16),
        compiler_params=pltpu.CompilerParams(dimension_semantics=("parallel",),
                                             vmem_limit_bytes=VMEM_LIMIT),
        name="decode",
    )(q, c_new, kpe2_new, cache_c, cache_pe_t, wukt, wuv)


def _pool_seq(hist, u):
    ext = jnp.concatenate([hist, u], axis=0)
    outs = []
    for g, w in enumerate(POOL_WINDOWS):
        e = ext[:, g * POOL_GROUP:(g + 1) * POOL_GROUP]
        s = e
        shift = 1
        while shift < w:
            s = s + pltpu.roll(s, shift, axis=0)
            shift *= 2
        outs.append(s[HIST_ROWS:] * (1.0 / w) - e[HIST_ROWS:])
    return outs


def _merge_body(*refs, n_seq, prompt):
    if prompt:
        (x_ref, o_ref, u_ref, hprev_ref, hmeta_ref, g_ref, wab_ref, wpg_ref, ps_ref, wpb_ref, wout_ref,
         gffn_ref, wup_ref, wdown_ref, gfin_ref, y_ref) = refs
        first = pl.program_id(1) == 0
        hists = [jnp.where(first, hmeta_ref[...], hprev_ref[...])]
    else:
        (x_ref, o_ref, u_ref, hist_ref, g_ref, wab_ref, wpg_ref, ps_ref, wpb_ref, wout_ref,
         gffn_ref, wup_ref, wdown_ref, gfin_ref, y_ref) = refs
        hists = [hist_ref[i] for i in range(n_seq)]

    rows = x_ref.shape[0]
    t = rows // n_seq
    halves = [slice(0, rows // 2), slice(rows // 2, rows)]
    a = [_mm(o_ref[hs, :], wab_ref[...]) for hs in halves]
    u = u_ref[...]
    per_seq = [_pool_seq(hists[i], u[i * t:(i + 1) * t]) for i in range(n_seq)]
    mixed = []
    for g in range(len(POOL_WINDOWS)):
        pooled = jnp.concatenate([per_seq[i][g] for i in range(n_seq)], axis=0) if n_seq > 1 else per_seq[0][g]
        mixed.append(_mm(pooled.astype(BF16), wpg_ref[g]))
    pm = (jnp.concatenate(mixed, axis=-1) * ps_ref[...]).astype(BF16)
    p = [_mm(pm[hs], wpb_ref[...]) for hs in halves]
    h = []
    for i, hs in enumerate(halves):
        gates = g_ref[hs, :].astype(F32)
        mix = (gates[:, :D_MODEL] * a[i] + gates[:, D_MODEL:] * p[i]).astype(BF16)
        h.append(x_ref[hs, :] + _mm(mix, wout_ref[...]))
    hn = [_rms(h_i, gffn_ref[...]).astype(BF16) for h_i in h]
    for c in range(D_FF // FF_CHUNK):
        cols = slice(c * FF_CHUNK, (c + 1) * FF_CHUNK)
        up = [jnp.maximum(_mm(hn_i, wup_ref[:, cols]), 0.0) for hn_i in hn]
        h = [h_i + _mm((up_i * up_i).astype(BF16), wdown_ref[cols, :]) for h_i, up_i in zip(h, up)]
    for hs, h_i in zip(halves, h):
        y_ref[hs, :] = _rms(h_i, gfin_ref[...])


def _merge_ffn(x, o, u, gates, hist_args, w, *, tile, n_seq, prompt):
    G, R, _ = x.shape
    nt = R // tile
    row = lambda width: pl.BlockSpec((None, tile, width), lambda g, t: (g, t, 0))
    if prompt:
        (u_meta,) = hist_args
        per_tile = tile // HIST_ROWS
        hist_specs = [
            pl.BlockSpec((None, HIST_ROWS, D_POOL), lambda g, t: (g, jnp.maximum(t * per_tile - 1, 0), 0)),
            pl.BlockSpec((HIST_ROWS, D_POOL), lambda g, t: (0, 0)),
        ]
        hist_in = [u, u_meta]
    else:
        (hist,) = hist_args
        hist_specs = [pl.BlockSpec((n_seq, HIST_ROWS, D_POOL), lambda g, t: (t, 0, 0))]
        hist_in = [hist]
    weights = (w["w_attn_br"], w["w_pool_grp"], w["pool_scale"], w["w_pool_br"], w["w_out"],
               w["g_ffn"], w["w_up"], w["w_down"], w["g_final"])
    in_specs = ([row(D_MODEL), row(ATTN_WIDTH), row(D_POOL)] + hist_specs + [row(2 * D_MODEL)]
                + [_const_spec(a.shape) for a in weights])
    return pl.pallas_call(
        functools.partial(_merge_body, n_seq=n_seq, prompt=prompt),
        grid=(G, nt),
        in_specs=in_specs,
        out_specs=row(D_MODEL),
        out_shape=jax.ShapeDtypeStruct((G, R, D_MODEL), F32),
        compiler_params=pltpu.CompilerParams(dimension_semantics=("parallel", "parallel"),
                                             vmem_limit_bytes=VMEM_LIMIT),
        name="merge_ffn",
    )(x, o, u, *hist_in, gates, *weights)


def _rope_tables(base_pos, off_pos):
    half = QK_ROPE // 2
    inv_freq = jnp.exp(-math.log(ROPE_BASE) * jnp.arange(half, dtype=jnp.float32) / half)
    reps = LANES // half
    freq = jnp.tile(inv_freq, reps)
    sign = jnp.tile(jnp.concatenate([-jnp.ones(half, F32), jnp.ones(half, F32)]), reps // 2)
    a = base_pos.astype(jnp.float32)[:, None, None] * freq
    b = off_pos.astype(jnp.float32)[None, :, None] * freq
    cos = (jnp.cos(a) * jnp.cos(b) - jnp.sin(a) * jnp.sin(b)).reshape(-1, LANES)
    sin = ((jnp.sin(a) * jnp.cos(b) + jnp.cos(a) * jnp.sin(b)) * sign).reshape(-1, LANES)
    a_t = inv_freq[:, None, None] * base_pos.astype(jnp.float32)[None, :, None]
    b_t = inv_freq[:, None, None] * off_pos.astype(jnp.float32)[None, None, :]
    cos_t = (jnp.cos(a_t) * jnp.cos(b_t) - jnp.sin(a_t) * jnp.sin(b_t)).reshape(half, -1)
    sin_t = (jnp.sin(a_t) * jnp.cos(b_t) + jnp.cos(a_t) * jnp.sin(b_t)).reshape(half, -1)
    return cos, sin, cos_t, sin_t


def _prep_weights(l, w_in, g_norm_mix, g_q, g_kv, w_q_up, w_uk, w_uv, w_attn_br, w_pool_grp,
                  pool_scale, w_pool_br, w_out, g_norm_ffn, w_up, w_down, g_final):
    half = QK_ROPE // 2
    i0, i1, i2, i3 = Q_LORA, Q_LORA + KV_LORA, Q_LORA + KV_LORA + QK_ROPE, Q_LORA + KV_LORA + QK_ROPE + D_POOL
    wi = w_in[l]
    w_kr = wi[:, i1:i2]
    w_kr_sw = jnp.concatenate([w_kr[:, half:], w_kr[:, :half]], axis=-1)
    wq = w_q_up[l].reshape(Q_LORA, N_HEADS, QK_NOPE + QK_ROPE)
    wq_r1, wq_r2 = wq[:, :, QK_NOPE:QK_NOPE + half], wq[:, :, QK_NOPE + half:]
    row = lambda v: v.reshape(1, -1).astype(F32)
    w_qup = jnp.concatenate([
        wq[:, :, :QK_NOPE].reshape(Q_LORA, -1),
        jnp.concatenate([wq_r1, wq_r2], axis=-1).reshape(Q_LORA, -1),
        jnp.concatenate([wq_r2, wq_r1], axis=-1).reshape(Q_LORA, -1)], axis=-1).astype(BF16)
    return {
        "g_mix": row(g_norm_mix[l]), "g_q": row(g_q[l]), "g_kv": row(g_kv[l]),
        "w_qlat": wi[:, :i0].astype(BF16),
        "w_kvr": jnp.concatenate([wi[:, i0:i1], w_kr, w_kr, w_kr_sw, w_kr_sw], axis=-1).astype(BF16),
        "w_u": wi[:, i2:i3].astype(BF16),
        "w_g": wi[:, i3:].astype(BF16),
        "w_qup": w_qup,
        "w_qup_t": w_qup[:, :N_HEADS * (QK_NOPE + QK_ROPE)].T,
        "w_uk": w_uk[l].reshape(KV_LORA, N_HEADS * QK_NOPE).astype(BF16),
        "w_uvt": w_uv[l].reshape(KV_LORA, N_HEADS * V_HEAD).T.astype(BF16),
        "w_ukt": jnp.transpose(w_uk[l], (1, 2, 0)).astype(BF16),
        "w_uv3": jnp.transpose(w_uv[l], (1, 0, 2)).astype(BF16),
        "w_attn_br": w_attn_br[l].astype(BF16),
        "w_pool_grp": w_pool_grp[l].astype(BF16),
        "pool_scale": row(pool_scale[l]),
        "w_pool_br": w_pool_br[l].astype(BF16),
        "w_out": w_out[l].astype(BF16),
        "g_ffn": row(g_norm_ffn[l]),
        "w_up": w_up[l].astype(BF16),
        "w_down": w_down[l].astype(BF16),
        "g_final": row(g_final),
    }


def kernel(x_prompt, x_sample, cache_kv_latent, cache_k_rope, cache_pool, meta_tokens, w_in, g_norm_mix, g_q, g_kv, w_q_up, w_uk, w_uv, w_attn_br, w_pool_grp, pool_scale, w_pool_br, w_out, g_norm_ffn, w_up, w_down, g_final):
    B, S, _ = x_prompt.shape
    Bd, T, _ = x_sample.shape
    past = cache_kv_latent.shape[2]
    assert w_in.shape[0] == 1 and S % ROW_TILE == 0 and (Bd * T) % ROW_TILE == 0 and ROW_TILE % T == 0
    assert N_META >= max(POOL_WINDOWS) and POOL_HIST + 1 >= max(POOL_WINDOWS) and T >= POOL_HIST
    w = _prep_weights(0, w_in, g_norm_mix, g_q, g_kv, w_q_up, w_uk, w_uv, w_attn_br, w_pool_grp,
                      pool_scale, w_pool_br, w_out, g_norm_ffn, w_up, w_down, g_final)

    meta = jnp.concatenate([meta_tokens.astype(F32), jnp.zeros((META_PAD - N_META, D_MODEL), F32)], axis=0)
    iota = functools.partial(jnp.arange, dtype=jnp.int32)
    rope_m = _rope_tables(jnp.zeros((1,), jnp.int32), iota(META_PAD))
    pm = _project(meta[None], rope_m, w, tile=META_PAD, emit_q=False, emit_kv=True,
                  emit_gates=False, emit_kpe2=False)

    rope_p = _rope_tables(N_META + ROW_TILE * iota(S // ROW_TILE), iota(ROW_TILE))
    pp = _project(x_prompt, rope_p, w, tile=ROW_TILE, emit_q="transposed", emit_kv=True,
                  emit_gates=True, emit_kpe2=False, lead_rows=N_META)
    o_p = _flash(pp["q"], pp["k"], pp["vt"], pm["k"], pm["vt"])
    y_prompt = _merge_ffn(x_prompt, o_p, pp["u"], pp["gates"], (pm["u"][0],), w,
                          tile=ROW_TILE, n_seq=1, prompt=True)

    rs = Bd * T
    rope_s = _rope_tables(jnp.full((Bd,), past, jnp.int32), iota(T))
    xs = x_sample.reshape(1, rs, D_MODEL)
    ps = _project(xs, rope_s, w, tile=ROW_TILE, emit_q=True, emit_kv=False,
                  emit_gates=True, emit_kpe2=True)
    cache_pe_t = jnp.swapaxes(cache_k_rope[0], 1, 2)
    o_s = _decode(ps["q"][0], ps["c"][0], ps["kpe2"][0], cache_kv_latent[0], cache_pe_t,
                  w["w_ukt"], w["w_uv3"], seq=T)
    hist_s = jnp.concatenate([jnp.zeros((Bd, HIST_ROWS - POOL_HIST, D_POOL), F32), cache_pool[0]], axis=1)
    y_sample = _merge_ffn(xs, o_s[None], ps["u"], ps["gates"], (hist_s,), w,
                          tile=ROW_TILE, n_seq=ROW_TILE // T, prompt=False).reshape(Bd, T, D_MODEL)

    def with_meta(m, f):
        return _fill_lead_rows(f, m[0, :N_META])[None]

    c_s = ps["c"].reshape(Bd, T, KV_LORA)
    pe_s = ps["kpe"].reshape(Bd, T, QK_ROPE)
    u_s = ps["u"].reshape(Bd, T, D_POOL)
    return (y_prompt, y_sample,
            with_meta(pm["c"], pp["c"]), with_meta(pm["kpe"], pp["kpe"]), pp["u"][:, -POOL_HIST:][None],
            c_s[None], pe_s[None], u_s[:, -POOL_HIST:][None])
```

```python
import functools
import math

import jax
import jax.numpy as jnp
from jax import lax
from jax.experimental import pallas as pl
from jax.experimental.pallas import tpu as pltpu

F32 = jnp.float32
BF16 = jnp.bfloat16

D_MODEL = 1024
N_HEADS = 8
Q_LORA = 384
KV_LORA = 256
QK_NOPE = 128
QK_ROPE = 64
V_HEAD = 128
ATTN_WIDTH = N_HEADS * V_HEAD
POOL_WINDOWS = (2, 4, 8, 16)
POOL_GROUP = 128
D_POOL = len(POOL_WINDOWS) * POOL_GROUP
POOL_HIST = max(POOL_WINDOWS) - 1
HIST_ROWS = POOL_HIST + 1
D_FF = 4 * D_MODEL
N_META = 16
CHUNK = 64
ROPE_BASE = 10000.0
EPS = 1e-6
SM_SCALE = (QK_NOPE + QK_ROPE) ** -0.5
LOG2E = 1.4426950408889634
NEG_INF = -1e30

LANES = 128
HEAD_W = QK_NOPE + LANES
V_ROWS = V_HEAD + 16
ROW_TILE = 512
META_PAD = 128
DECODE_SEQS = 4
Q_STREAMS = 4
FF_CHUNK = 1024
VMEM_LIMIT = 56 * 1024 * 1024


def _nt(a, b):
    return lax.dot_general(a, b, (((1,), (1,)), ((), ())), preferred_element_type=F32)


def _mm(a, b):
    return jnp.dot(a, b, preferred_element_type=F32)


def _rms(x, g):
    return x * lax.rsqrt(jnp.mean(x * x, axis=-1, keepdims=True) + EPS) * g


def _const_spec(shape):
    nd = len(shape)
    return pl.BlockSpec(shape, lambda *_: (0,) * nd, pipeline_mode=pl.Buffered(1))


def _project_body(x_ref, cos_ref, sin_ref, *rest, emit_q, emit_kv, emit_gates, emit_kpe2):
    rest = list(rest)
    cos_t_ref, sin_t_ref = (rest.pop(0), rest.pop(0)) if emit_q == "transposed" else (None, None)
    gmix_ref, gq_ref, gkv_ref, wql_ref, wkvr_ref, wu_ref, wg_ref, wqup_ref, wuk_ref, wuvt_ref = rest[:10]
    outs = rest[10:]
    q_ref = outs.pop(0) if emit_q else None
    k_ref, vt_ref = (outs.pop(0), outs.pop(0)) if emit_kv else (None, None)
    c_ref, kpe_ref, u_ref = outs.pop(0), outs.pop(0), outs.pop(0)
    if len(c_ref.shape) == 3:
        c_ref, kpe_ref = c_ref.at[0], kpe_ref.at[0]
    g_ref = outs.pop(0) if emit_gates else None
    kpe2_ref = outs.pop(0) if emit_kpe2 else None

    rows = x_ref.shape[0]
    xn = _rms(x_ref[...], gmix_ref[...]).astype(BF16)
    cos = cos_ref[...]
    sin = sin_ref[...]
    lane = lax.broadcasted_iota(jnp.int32, (rows, LANES), 1)
    half_masks = (lane < QK_ROPE, lane >= QK_ROPE)

    q_lat = _mm(xn, wql_ref[...]) if emit_q else None

    kvr = _mm(xn, wkvr_ref[...])
    c = _rms(kvr[:, :KV_LORA], gkv_ref[...])
    c_ref[...] = c
    kpe2 = kvr[:, KV_LORA:KV_LORA + LANES] * cos + kvr[:, KV_LORA + LANES:] * sin
    kpe_ref[...] = kpe2[:, :QK_ROPE]
    if emit_kpe2:
        kpe2_ref[...] = kpe2.astype(BF16)
    cb = c.astype(BF16)

    nope_w = N_HEADS * QK_NOPE
    rope_w = N_HEADS * QK_ROPE
    if emit_q == "transposed":
        qn = _rms(q_lat, gq_ref[...]).astype(BF16)
        qall = _nt(wqup_ref[...], qn) * (SM_SCALE * LOG2E)
        half = QK_ROPE // 2
        c_t = cos_t_ref[...]
        s_t = sin_t_ref[...]
        zeros = jnp.zeros((QK_ROPE, rows), BF16)
        for h in range(N_HEADS):
            r1 = qall[nope_w + h * QK_ROPE:nope_w + h * QK_ROPE + half]
            r2 = qall[nope_w + h * QK_ROPE + half:nope_w + (h + 1) * QK_ROPE]
            r = jnp.concatenate([r1 * c_t - r2 * s_t, r1 * s_t + r2 * c_t], axis=0).astype(BF16)
            own = h * HEAD_W + QK_NOPE + (h % 2) * QK_ROPE
            other = h * HEAD_W + QK_NOPE + (1 - h % 2) * QK_ROPE
            q_ref[h * HEAD_W:h * HEAD_W + QK_NOPE, :] = qall[h * QK_NOPE:(h + 1) * QK_NOPE].astype(BF16)
            q_ref[own:own + QK_ROPE, :] = r
            q_ref[other:other + QK_ROPE, :] = zeros
    elif emit_q:
        qn = _rms(q_lat, gq_ref[...]).astype(BF16)
        qall = _mm(qn, wqup_ref[...]) * (SM_SCALE * LOG2E)
        for h in range(N_HEADS):
            j = h // 2
            r = qall[:, nope_w + j * LANES:nope_w + (j + 1) * LANES] * cos
            r = r + qall[:, nope_w + rope_w + j * LANES:nope_w + rope_w + (j + 1) * LANES] * sin
            q_ref[:, h * HEAD_W:h * HEAD_W + QK_NOPE] = qall[:, h * QK_NOPE:(h + 1) * QK_NOPE].astype(BF16)
            q_ref[:, h * HEAD_W + QK_NOPE:(h + 1) * HEAD_W] = jnp.where(half_masks[h % 2], r, 0.0).astype(BF16)

    if emit_kv:
        k_nope = _mm(cb, wuk_ref[...])
        for h in range(N_HEADS):
            k_ref[:, h * HEAD_W:h * HEAD_W + QK_NOPE] = k_nope[:, h * QK_NOPE:(h + 1) * QK_NOPE].astype(BF16)
            k_ref[:, h * HEAD_W + QK_NOPE:(h + 1) * HEAD_W] = jnp.where(half_masks[h % 2], kpe2, 0.0).astype(BF16)
        vt = _nt(wuvt_ref[...], cb)
        extra = V_ROWS - V_HEAD
        ones_rows = jnp.where(lax.broadcasted_iota(jnp.int32, (extra, rows), 0) == 0, 1.0, 0.0).astype(BF16)
        for h in range(N_HEADS):
            vt_ref[h * V_ROWS:h * V_ROWS + V_HEAD, :] = vt[h * V_HEAD:(h + 1) * V_HEAD].astype(BF16)
            vt_ref[h * V_ROWS + V_HEAD:(h + 1) * V_ROWS, :] = ones_rows

    if emit_gates:
        logits = _mm(xn, wg_ref[...])
        g_ref[...] = (1.0 / (1.0 + jnp.exp(-logits))).astype(BF16)

    u_ref[...] = _mm(xn, wu_ref[...])


def _project(x, rope, w, *, tile, emit_q, emit_kv, emit_gates, emit_kpe2, lead_rows=0):
    cos, sin, cos_t, sin_t = rope
    G, R, _ = x.shape
    nt = R // tile
    assert nt * tile == R
    row = lambda width: pl.BlockSpec((None, tile, width), lambda g, t: (g, t, 0))
    tab = pl.BlockSpec((tile, LANES), lambda g, t: (t, 0))
    q_t = emit_q == "transposed"
    weights = (w["g_mix"], w["g_q"], w["g_kv"], w["w_qlat"], w["w_kvr"], w["w_u"], w["w_g"],
               w["w_qup_t"] if q_t else w["w_qup"], w["w_uk"], w["w_uvt"])
    tables = [cos, sin]
    in_specs = [row(D_MODEL), tab, tab]
    if q_t:
        tables += [cos_t, sin_t]
        in_specs += [pl.BlockSpec((QK_ROPE // 2, tile), lambda g, t: (0, t))] * 2
    in_specs += [_const_spec(a.shape) for a in weights]
    out_shape, out_specs = [], []

    def add(shape, spec, dtype):
        out_shape.append(jax.ShapeDtypeStruct(shape, dtype))
        out_specs.append(spec)

    if q_t:
        add((G, N_HEADS * HEAD_W, R), pl.BlockSpec((None, N_HEADS * HEAD_W, tile), lambda g, t: (g, 0, t)), BF16)
    elif emit_q:
        add((G, R, N_HEADS * HEAD_W), row(N_HEADS * HEAD_W), BF16)
    if emit_kv:
        add((G, nt, tile, N_HEADS * HEAD_W),
            pl.BlockSpec((None, None, tile, N_HEADS * HEAD_W), lambda g, t: (g, t, 0, 0)), BF16)
        add((G, nt, N_HEADS * V_ROWS, tile),
            pl.BlockSpec((None, None, N_HEADS * V_ROWS, tile), lambda g, t: (g, t, 0, 0)), BF16)
    cache = row if lead_rows == 0 else (lambda width: pl.BlockSpec(
        (pl.Element(1), pl.Element(tile), pl.Element(width)),
        lambda g, t: (g, pl.multiple_of(lead_rows + t * tile, math.gcd(lead_rows, tile)), 0)))
    add((G, lead_rows + R, KV_LORA), cache(KV_LORA), F32)
    add((G, lead_rows + R, QK_ROPE), cache(QK_ROPE), F32)
    add((G, R, D_POOL), row(D_POOL), F32)
    if emit_gates:
        add((G, R, 2 * D_MODEL), row(2 * D_MODEL), BF16)
    if emit_kpe2:
        add((G, R, LANES), row(LANES), BF16)

    body = functools.partial(_project_body, emit_q=emit_q, emit_kv=emit_kv, emit_gates=emit_gates,
                             emit_kpe2=emit_kpe2)
    outs = pl.pallas_call(
        body,
        grid=(G, nt),
        in_specs=in_specs,
        out_specs=out_specs,
        out_shape=out_shape,
        compiler_params=pltpu.CompilerParams(dimension_semantics=("parallel", "parallel"),
                                             vmem_limit_bytes=VMEM_LIMIT),
        name="project",
    )(x, *tables, *weights)
    outs = list(outs)
    res = {}
    if emit_q:
        res["q"] = outs.pop(0)
    if emit_kv:
        res["k"], res["vt"] = outs.pop(0), outs.pop(0)
    res["c"], res["kpe"], res["u"] = outs.pop(0), outs.pop(0), outs.pop(0)
    if emit_gates:
        res["gates"] = outs.pop(0)
    if emit_kpe2:
        res["kpe2"] = outs.pop(0)
    return res


def _fill_lead_body(big_ref, rows_ref, out_ref):
    del big_ref
    out_ref[...] = rows_ref[...]


def _fill_lead_rows(big, rows):
    G, _, W = big.shape
    n = rows.shape[0]
    return pl.pallas_call(
        _fill_lead_body,
        grid=(G,),
        in_specs=[pl.BlockSpec(memory_space=pl.ANY), pl.BlockSpec((n, W), lambda g: (0, 0))],
        out_specs=pl.BlockSpec((None, n, W), lambda g: (g, 0, 0)),
        out_shape=jax.ShapeDtypeStruct(big.shape, big.dtype),
        input_output_aliases={0: 0},
        name="fill_lead_rows",
    )(big, rows)


def _flash_body(qt_ref, k_ref, vt_ref, km_ref, vtm_ref, o_ref, m_ref, acc_ref, s_ref, cm_ref, *, streams):
    tk = k_ref.shape[1]
    base = pl.program_id(2) * streams

    def produce(buf, kb, ss, slot=None):
        for s in ss:
            dst = s if slot is None else slot
            s_t = _mm(k_ref[kb], qt_ref[:, s * tk:(s + 1) * tk])
            s_ref[buf, dst] = s_t
            cm_ref[buf, dst] = jnp.max(s_t, axis=0, keepdims=True)

    def own_columns(buf, s):
        diag = (lax.broadcasted_iota(jnp.int32, (LANES, LANES), 0) // CHUNK
                <= lax.broadcasted_iota(jnp.int32, (LANES, LANES), 1) // CHUNK)
        cols = []
        for j in range(tk // LANES):
            tiles = [s_ref[buf, s, i * LANES:(i + 1) * LANES, j * LANES:(j + 1) * LANES] for i in range(j + 1)]
            tiles[j] = jnp.where(diag, tiles[j], NEG_INF)
            cols.append(jnp.concatenate(tiles, axis=0))
        return cols

    def consume(buf, s, kb, own=False, slot=None):
        slot = s if slot is None else slot
        m_old = m_ref[s]
        if own:
            cols = own_columns(buf, slot)
            m_blk = jnp.concatenate([jnp.max(c, axis=0, keepdims=True) for c in cols], axis=1)
            m_new = jnp.maximum(m_old, m_blk)
            p_cols = []
            for j, c in enumerate(cols):
                p_j = jnp.exp2(c - m_new[:, j * LANES:(j + 1) * LANES]).astype(BF16)
                if c.shape[0] < tk:
                    p_j = jnp.concatenate([p_j, jnp.zeros((tk - c.shape[0], LANES), BF16)], axis=0)
                p_cols.append(p_j)
            p = jnp.concatenate(p_cols, axis=1)
        else:
            m_new = jnp.maximum(m_old, cm_ref[buf, slot])
            p = jnp.exp2(s_ref[buf, slot] - m_new).astype(BF16)
        acc_ref[s] = jnp.exp2(m_old - m_new) * acc_ref[s] + _mm(vt_ref[kb], p)
        m_ref[s] = m_new

    everyone = range(streams)

    s_m = _mm(km_ref[...], qt_ref[...])
    produce(0, 0, everyone)
    s_m = jnp.where(lax.broadcasted_iota(jnp.int32, s_m.shape, 0) < N_META, s_m, NEG_INF)
    m_0 = jnp.max(s_m, axis=0, keepdims=True)
    acc_0 = _mm(vtm_ref[...], jnp.exp2(s_m - m_0).astype(BF16))
    for s in everyone:
        m_ref[s] = m_0[:, s * tk:(s + 1) * tk]
        acc_ref[s] = acc_0[:, s * tk:(s + 1) * tk]

    def step(rd, wr, kb):
        lead = 0
        for i in range(streams + lead):
            if i < streams:
                produce(wr, kb + 1, [i])
            if i >= lead:
                consume(rd, i - lead, kb)

    per_trip = 4 if streams % 4 == 0 else 2

    def trip(j, carry):
        for i in range(per_trip):
            step(i % 2, 1 - i % 2, per_trip * j + i)
        return carry

    lax.fori_loop(0, base // per_trip, trip, 0)

    def finish(s):
        acc = acc_ref[s]
        o_ref[s * tk:(s + 1) * tk, :] = (acc[:V_HEAD] / acc[V_HEAD:V_HEAD + 1]).T.astype(BF16)

    for s in range(1, streams):
        produce(1, base + s, [s])
        consume(0, s - 1, base, own=(s == 1))
    consume(0, streams - 1, base)
    finish(0)
    left = [(s, base + j) for s in range(2, streams) for j in range(1, s)]
    assert len(left) <= streams
    for slot, (s, kb) in enumerate(left):
        produce(0, kb, [s], slot=slot)
        if slot + 1 < streams:
            consume(1, slot + 1, base + slot + 1, own=True)
    for s in range(len(left) + 1, streams):
        consume(1, s, base + s, own=True)
    finish(1)
    for slot, (s, kb) in enumerate(left):
        consume(0, s, kb, slot=slot)
        if slot + 1 == len(left) or left[slot + 1][0] != s:
            finish(s)


def _flash(q, k, vt, k_meta, vt_meta):
    B, _, S = q.shape
    nt, tile = k.shape[1], k.shape[2]
    streams = Q_STREAMS
    assert nt % streams == 0 and streams % 2 == 0
    return pl.pallas_call(
        functools.partial(_flash_body, streams=streams),
        grid=(B, N_HEADS, nt // streams),
        in_specs=[
            pl.BlockSpec((None, HEAD_W, streams * tile), lambda b, h, i: (b, h, i)),
            pl.BlockSpec((None, nt, tile, HEAD_W), lambda b, h, i: (b, 0, 0, h)),
            pl.BlockSpec((None, nt, V_ROWS, tile), lambda b, h, i: (b, 0, h, 0)),
            pl.BlockSpec((None, None, META_PAD, HEAD_W), lambda b, h, i: (0, 0, 0, h)),
            pl.BlockSpec((None, None, V_ROWS, META_PAD), lambda b, h, i: (0, 0, h, 0)),
        ],
        out_specs=pl.BlockSpec((None, streams * tile, V_HEAD), lambda b, h, i: (b, i, h)),
        out_shape=jax.ShapeDtypeStruct((B, S, ATTN_WIDTH), BF16),
        scratch_shapes=[pltpu.VMEM((streams, 1, tile), F32),
                        pltpu.VMEM((streams, V_ROWS, tile), F32),
                        pltpu.VMEM((2, streams, tile, tile), F32), pltpu.VMEM((2, streams, 1, tile), F32)],
        compiler_params=pltpu.CompilerParams(dimension_semantics=("parallel", "parallel", "arbitrary"),
                                             vmem_limit_bytes=VMEM_LIMIT),
        name="flash",
    )(q, k, vt, k_meta, vt_meta)


def _decode_body(q_ref, cn_ref, pn_ref, cc_ref, pc_ref, wukt_ref, wuv_ref, o_ref):
    n = cc_ref.shape[0]
    t = q_ref.shape[0] // n
    seqs = range(n)
    rows = [slice(i * t, (i + 1) * t) for i in seqs]
    q = [q_ref[r, :] for r in rows]
    qa = [jnp.concatenate([_mm(q[i][:, h * HEAD_W:h * HEAD_W + QK_NOPE], wukt_ref[h]) for h in range(N_HEADS)],
                          axis=0).astype(BF16) for i in seqs]
    qp = [jnp.concatenate([q[i][:, h * HEAD_W + QK_NOPE:(h + 1) * HEAD_W] for h in range(N_HEADS)], axis=0)
          for i in seqs]
    cc = [cc_ref[i].astype(BF16) for i in seqs]
    cn = [cn_ref[r, :].astype(BF16) for r in rows]
    pc2 = [jnp.concatenate([pc_ref[i].astype(BF16)] * (LANES // QK_ROPE), axis=0) for i in seqs]
    s_c = [_nt(qa[i], cc[i]) + _mm(qp[i], pc2[i]) for i in seqs]
    s_n = [_nt(qa[i], cn[i]) + _nt(qp[i], pn_ref[rows[i], :]) for i in seqs]
    o_lat = []
    for i in seqs:
        m = jnp.maximum(jnp.max(s_c[i], axis=-1, keepdims=True), jnp.max(s_n[i], axis=-1, keepdims=True))
        p_c = jnp.exp2(s_c[i] - m)
        p_n = jnp.exp2(s_n[i] - m)
        l = jnp.sum(p_c, axis=-1, keepdims=True) + jnp.sum(p_n, axis=-1, keepdims=True)
        o_lat.append(((_mm(p_c.astype(BF16), cc[i]) + _mm(p_n.astype(BF16), cn[i])) / l).astype(BF16))
    for i in seqs:
        for h in range(N_HEADS):
            o_ref[rows[i], h * V_HEAD:(h + 1) * V_HEAD] = _mm(o_lat[i][h * t:(h + 1) * t], wuv_ref[h]).astype(BF16)


def _decode(q, c_new, kpe2_new, cache_c, cache_pe_t, wukt, wuv, *, seq):
    R = q.shape[0]
    nb, past, _ = cache_c.shape
    per = DECODE_SEQS
    assert nb * seq == R and nb % per == 0
    return pl.pallas_call(
        _decode_body,
        grid=(nb // per,),
        in_specs=[
            pl.BlockSpec((per * seq, N_HEADS * HEAD_W), lambda i: (i, 0)),
            pl.BlockSpec((per * seq, KV_LORA), lambda i: (i, 0)),
            pl.BlockSpec((per * seq, LANES), lambda i: (i, 0)),
            pl.BlockSpec((per, past, KV_LORA), lambda i: (i, 0, 0)),
            pl.BlockSpec((per, QK_ROPE, past), lambda i: (i, 0, 0)),
            _const_spec(wukt.shape),
            _const_spec(wuv.shape),
        ],
        out_specs=pl.BlockSpec((per * seq, ATTN_WIDTH), lambda i: (i, 0)),
        out_shape=jax.ShapeDtypeStruct((R, ATTN_WIDTH), BF16),
        compiler_params=pltpu.CompilerParams(dimension_semantics=("parallel",),
                                             vmem_limit_bytes=VMEM_LIMIT),
        name="decode",
    )(q, c_new, kpe2_new, cache_c, cache_pe_t, wukt, wuv)


def _pool_seq(hist, u):
    ext = jnp.concatenate([hist, u], axis=0)
    outs = []
    for g, w in enumerate(POOL_WINDOWS):
        e = ext[:, g * POOL_GROUP:(g + 1) * POOL_GROUP]
        s = e
        shift = 1
        while shift < w:
            s = s + pltpu.roll(s, shift, axis=0)
            shift *= 2
        outs.append(s[HIST_ROWS:] * (1.0 / w) - e[HIST_ROWS:])
    return outs


def _merge_body(*refs, n_seq, prompt):
    if prompt:
        (x_ref, o_ref, u_ref, hprev_ref, hmeta_ref, g_ref, wab_ref, wpg_ref, ps_ref, wpb_ref, wout_ref,
         gffn_ref, wup_ref, wdown_ref, gfin_ref, y_ref) = refs
        first = pl.program_id(1) == 0
        hists = [jnp.where(first, hmeta_ref[...], hprev_ref[...])]
    else:
        (x_ref, o_ref, u_ref, hist_ref, g_ref, wab_ref, wpg_ref, ps_ref, wpb_ref, wout_ref,
         gffn_ref, wup_ref, wdown_ref, gfin_ref, y_ref) = refs
        hists = [hist_ref[i] for i in range(n_seq)]

    rows = x_ref.shape[0]
    t = rows // n_seq
    halves = [slice(0, rows // 2), slice(rows // 2, rows)]
    a = [_mm(o_ref[hs, :], wab_ref[...]) for hs in halves]
    u = u_ref[...]
    per_seq = [_pool_seq(hists[i], u[i * t:(i + 1) * t]) for i in range(n_seq)]
    mixed = []
    for g in range(len(POOL_WINDOWS)):
        pooled = jnp.concatenate([per_seq[i][g] for i in range(n_seq)], axis=0) if n_seq > 1 else per_seq[0][g]
        mixed.append(_mm(pooled.astype(BF16), wpg_ref[g]))
    pm = (jnp.concatenate(mixed, axis=-1) * ps_ref[...]).astype(BF16)
    p = [_mm(pm[hs], wpb_ref[...]) for hs in halves]
    h = []
    for i, hs in enumerate(halves):
        gates = g_ref[hs, :].astype(F32)
        mix = (gates[:, :D_MODEL] * a[i] + gates[:, D_MODEL:] * p[i]).astype(BF16)
        h.append(x_ref[hs, :] + _mm(mix, wout_ref[...]))
    hn = [_rms(h_i, gffn_ref[...]).astype(BF16) for h_i in h]
    for c in range(D_FF // FF_CHUNK):
        cols = slice(c * FF_CHUNK, (c + 1) * FF_CHUNK)
        up = [jnp.maximum(_mm(hn_i, wup_ref[:, cols]), 0.0) for hn_i in hn]
        h = [h_i + _mm((up_i * up_i).astype(BF16), wdown_ref[cols, :]) for h_i, up_i in zip(h, up)]
    for hs, h_i in zip(halves, h):
        y_ref[hs, :] = _rms(h_i, gfin_ref[...])


def _merge_ffn(x, o, u, gates, hist_args, w, *, tile, n_seq, prompt):
    G, R, _ = x.shape
    nt = R // tile
    row = lambda width: pl.BlockSpec((None, tile, width), lambda g, t: (g, t, 0))
    if prompt:
        (u_meta,) = hist_args
        per_tile = tile // HIST_ROWS
        hist_specs = [
            pl.BlockSpec((None, HIST_ROWS, D_POOL), lambda g, t: (g, jnp.maximum(t * per_tile - 1, 0), 0)),
            pl.BlockSpec((HIST_ROWS, D_POOL), lambda g, t: (0, 0)),
        ]
        hist_in = [u, u_meta]
    else:
        (hist,) = hist_args
        hist_specs = [pl.BlockSpec((n_seq, HIST_ROWS, D_POOL), lambda g, t: (t, 0, 0))]
        hist_in = [hist]
    weights = (w["w_attn_br"], w["w_pool_grp"], w["pool_scale"], w["w_pool_br"], w["w_out"],
               w["g_ffn"], w["w_up"], w["w_down"], w["g_final"])
    in_specs = ([row(D_MODEL), row(ATTN_WIDTH), row(D_POOL)] + hist_specs + [row(2 * D_MODEL)]
                + [_const_spec(a.shape) for a in weights])
    return pl.pallas_call(
        functools.partial(_merge_body, n_seq=n_seq, prompt=prompt),
        grid=(G, nt),
        in_specs=in_specs,
        out_specs=row(D_MODEL),
        out_shape=jax.ShapeDtypeStruct((G, R, D_MODEL), F32),
        compiler_params=pltpu.CompilerParams(dimension_semantics=("parallel", "parallel"),
                                             vmem_limit_bytes=VMEM_LIMIT),
        name="merge_ffn",
    )(x, o, u, *hist_in, gates, *weights)


def _rope_tables(base_pos, off_pos):
    half = QK_ROPE // 2
    inv_freq = jnp.exp(-math.log(ROPE_BASE) * jnp.arange(half, dtype=jnp.float32) / half)
    reps = LANES // half
    freq = jnp.tile(inv_freq, reps)
    sign = jnp.tile(jnp.concatenate([-jnp.ones(half, F32), jnp.ones(half, F32)]), reps // 2)
    a = base_pos.astype(jnp.float32)[:, None, None] * freq
    b = off_pos.astype(jnp.float32)[None, :, None] * freq
    cos = (jnp.cos(a) * jnp.cos(b) - jnp.sin(a) * jnp.sin(b)).reshape(-1, LANES)
    sin = ((jnp.sin(a) * jnp.cos(b) + jnp.cos(a) * jnp.sin(b)) * sign).reshape(-1, LANES)
    a_t = inv_freq[:, None, None] * base_pos.astype(jnp.float32)[None, :, None]
    b_t = inv_freq[:, None, None] * off_pos.astype(jnp.float32)[None, None, :]
    cos_t = (jnp.cos(a_t) * jnp.cos(b_t) - jnp.sin(a_t) * jnp.sin(b_t)).reshape(half, -1)
    sin_t = (jnp.sin(a_t) * jnp.cos(b_t) + jnp.cos(a_t) * jnp.sin(b_t)).reshape(half, -1)
    return cos, sin, cos_t, sin_t


def _prep_weights(l, w_in, g_norm_mix, g_q, g_kv, w_q_up, w_uk, w_uv, w_attn_br, w_pool_grp,
                  pool_scale, w_pool_br, w_out, g_norm_ffn, w_up, w_down, g_final):
    half = QK_ROPE // 2
    i0, i1, i2, i3 = Q_LORA, Q_LORA + KV_LORA, Q_LORA + KV_LORA + QK_ROPE, Q_LORA + KV_LORA + QK_ROPE + D_POOL
    wi = w_in[l]
    w_kr = wi[:, i1:i2]
    w_kr_sw = jnp.concatenate([w_kr[:, half:], w_kr[:, :half]], axis=-1)
    wq = w_q_up[l].reshape(Q_LORA, N_HEADS, QK_NOPE + QK_ROPE)
    wq_r1, wq_r2 = wq[:, :, QK_NOPE:QK_NOPE + half], wq[:, :, QK_NOPE + half:]
    row = lambda v: v.reshape(1, -1).astype(F32)
    w_qup = jnp.concatenate([
        wq[:, :, :QK_NOPE].reshape(Q_LORA, -1),
        jnp.concatenate([wq_r1, wq_r2], axis=-1).reshape(Q_LORA, -1),
        jnp.concatenate([wq_r2, wq_r1], axis=-1).reshape(Q_LORA, -1)], axis=-1).astype(BF16)
    return {
        "g_mix": row(g_norm_mix[l]), "g_q": row(g_q[l]), "g_kv": row(g_kv[l]),
        "w_qlat": wi[:, :i0].astype(BF16),
        "w_kvr": jnp.concatenate([wi[:, i0:i1], w_kr, w_kr, w_kr_sw, w_kr_sw], axis=-1).astype(BF16),
        "w_u": wi[:, i2:i3].astype(BF16),
        "w_g": wi[:, i3:].astype(BF16),
        "w_qup": w_qup,
        "w_qup_t": w_qup[:, :N_HEADS * (QK_NOPE + QK_ROPE)].T,
        "w_uk": w_uk[l].reshape(KV_LORA, N_HEADS * QK_NOPE).astype(BF16),
        "w_uvt": w_uv[l].reshape(KV_LORA, N_HEADS * V_HEAD).T.astype(BF16),
        "w_ukt": jnp.transpose(w_uk[l], (1, 2, 0)).astype(BF16),
        "w_uv3": jnp.transpose(w_uv[l], (1, 0, 2)).astype(BF16),
        "w_attn_br": w_attn_br[l].astype(BF16),
        "w_pool_grp": w_pool_grp[l].astype(BF16),
        "pool_scale": row(pool_scale[l]),
        "w_pool_br": w_pool_br[l].astype(BF16),
        "w_out": w_out[l].astype(BF16),
        "g_ffn": row(g_norm_ffn[l]),
        "w_up": w_up[l].astype(BF16),
        "w_down": w_down[l].astype(BF16),
        "g_final": row(g_final),
    }


def kernel(x_prompt, x_sample, cache_kv_latent, cache_k_rope, cache_pool, meta_tokens, w_in, g_norm_mix, g_q, g_kv, w_q_up, w_uk, w_uv, w_attn_br, w_pool_grp, pool_scale, w_pool_br, w_out, g_norm_ffn, w_up, w_down, g_final):
    B, S, _ = x_prompt.shape
    Bd, T, _ = x_sample.shape
    past = cache_kv_latent.shape[2]
    assert w_in.shape[0] == 1 and S % ROW_TILE == 0 and (Bd * T) % ROW_TILE == 0 and ROW_TILE % T == 0
    assert N_META >= max(POOL_WINDOWS) and POOL_HIST + 1 >= max(POOL_WINDOWS) and T >= POOL_HIST
    w = _prep_weights(0, w_in, g_norm_mix, g_q, g_kv, w_q_up, w_uk, w_uv, w_attn_br, w_pool_grp,
                      pool_scale, w_pool_br, w_out, g_norm_ffn, w_up, w_down, g_final)

    meta = jnp.concatenate([meta_tokens.astype(F32), jnp.zeros((META_PAD - N_META, D_MODEL), F32)], axis=0)
    iota = functools.partial(jnp.arange, dtype=jnp.int32)
    rope_m = _rope_tables(jnp.zeros((1,), jnp.int32), iota(META_PAD))
    pm = _project(meta[None], rope_m, w, tile=META_PAD, emit_q=False, emit_kv=True,
                  emit_gates=False, emit_kpe2=False)

    rope_p = _rope_tables(N_META + ROW_TILE * iota(S // ROW_TILE), iota(ROW_TILE))
    pp = _project(x_prompt, rope_p, w, tile=ROW_TILE, emit_q="transposed", emit_kv=True,
                  emit_gates=True, emit_kpe2=False, lead_rows=N_META)
    o_p = _flash(pp["q"], pp["k"], pp["vt"], pm["k"], pm["vt"])
    y_prompt = _merge_ffn(x_prompt, o_p, pp["u"], pp["gates"], (pm["u"][0],), w,
                          tile=ROW_TILE, n_seq=1, prompt=True)

    rs = Bd * T
    rope_s = _rope_tables(jnp.full((Bd,), past, jnp.int32), iota(T))
    xs = x_sample.reshape(1, rs, D_MODEL)
    ps = _project(xs, rope_s, w, tile=ROW_TILE, emit_q=True, emit_kv=False,
                  emit_gates=True, emit_kpe2=True)
    cache_pe_t = jnp.swapaxes(cache_k_rope[0], 1, 2)
    o_s = _decode(ps["q"][0], ps["c"][0], ps["kpe2"][0], cache_kv_latent[0], cache_pe_t,
                  w["w_ukt"], w["w_uv3"], seq=T)
    hist_s = jnp.concatenate([jnp.zeros((Bd, HIST_ROWS - POOL_HIST, D_POOL), F32), cache_pool[0]], axis=1)
    y_sample = _merge_ffn(xs, o_s[None], ps["u"], ps["gates"], (hist_s,), w,
                          tile=ROW_TILE, n_seq=ROW_TILE // T, prompt=False).reshape(Bd, T, D_MODEL)

    def with_meta(m, f):
        return _fill_lead_rows(f, m[0, :N_META])[None]

    c_s = ps["c"].reshape(Bd, T, KV_LORA)
    pe_s = ps["kpe"].reshape(Bd, T, QK_ROPE)
    u_s = ps["u"].reshape(Bd, T, D_POOL)
    return (y_prompt, y_sample,
            with_meta(pm["c"], pp["c"]), with_meta(pm["kpe"], pp["kpe"]), pp["u"][:, -POOL_HIST:][None],
            c_s[None], pe_s[None], u_s[:, -POOL_HIST:][None])
```

```python
import functools
import math

import jax
import jax.numpy as jnp
from jax import lax
from jax.experimental import pallas as pl
from jax.experimental.pallas import tpu as pltpu

F32 = jnp.float32
BF16 = jnp.bfloat16

D_MODEL = 1024
N_HEADS = 8
Q_LORA = 384
KV_LORA = 256
QK_NOPE = 128
QK_ROPE = 64
V_HEAD = 128
ATTN_WIDTH = N_HEADS * V_HEAD
POOL_WINDOWS = (2, 4, 8, 16)
POOL_GROUP = 128
D_POOL = len(POOL_WINDOWS) * POOL_GROUP
POOL_HIST = max(POOL_WINDOWS) - 1
HIST_ROWS = POOL_HIST + 1
D_FF = 4 * D_MODEL
N_META = 16
CHUNK = 64
ROPE_BASE = 10000.0
EPS = 1e-6
SM_SCALE = (QK_NOPE + QK_ROPE) ** -0.5
LOG2E = 1.4426950408889634
NEG_INF = -1e30

LANES = 128
HEAD_W = QK_NOPE + LANES
V_ROWS = V_HEAD + 16
ROW_TILE = 512
META_PAD = 128
DECODE_SEQS = 4
Q_STREAMS = 4
FF_CHUNK = 1024
VMEM_LIMIT = 56 * 1024 * 1024


def _nt(a, b):
    return lax.dot_general(a, b, (((1,), (1,)), ((), ())), preferred_element_type=F32)


def _mm(a, b):
    return jnp.dot(a, b, preferred_element_type=F32)


def _rms(x, g):
    return x * lax.rsqrt(jnp.mean(x * x, axis=-1, keepdims=True) + EPS) * g


def _const_spec(shape):
    nd = len(shape)
    return pl.BlockSpec(shape, lambda *_: (0,) * nd, pipeline_mode=pl.Buffered(1))


def _project_body(x_ref, cos_ref, sin_ref, *rest, emit_q, emit_kv, emit_gates, emit_kpe2):
    rest = list(rest)
    cos_t_ref, sin_t_ref = (rest.pop(0), rest.pop(0)) if emit_q == "transposed" else (None, None)
    gmix_ref, gq_ref, gkv_ref, wql_ref, wkvr_ref, wu_ref, wg_ref, wqup_ref, wuk_ref, wuvt_ref = rest[:10]
    outs = rest[10:]
    q_ref = outs.pop(0) if emit_q else None
    k_ref, vt_ref = (outs.pop(0), outs.pop(0)) if emit_kv else (None, None)
    c_ref, kpe_ref, u_ref = outs.pop(0), outs.pop(0), outs.pop(0)
    if len(c_ref.shape) == 3:
        c_ref, kpe_ref = c_ref.at[0], kpe_ref.at[0]
    g_ref = outs.pop(0) if emit_gates else None
    kpe2_ref = outs.pop(0) if emit_kpe2 else None

    rows = x_ref.shape[0]
    xn = _rms(x_ref[...], gmix_ref[...]).astype(BF16)
    cos = cos_ref[...]
    sin = sin_ref[...]
    lane = lax.broadcasted_iota(jnp.int32, (rows, LANES), 1)
    half_masks = (lane < QK_ROPE, lane >= QK_ROPE)

    q_lat = _mm(xn, wql_ref[...]) if emit_q else None

    kvr = _mm(xn, wkvr_ref[...])
    c = _rms(kvr[:, :KV_LORA], gkv_ref[...])
    c_ref[...] = c
    kpe2 = kvr[:, KV_LORA:KV_LORA + LANES] * cos + kvr[:, KV_LORA + LANES:] * sin
    kpe_ref[...] = kpe2[:, :QK_ROPE]
    if emit_kpe2:
        kpe2_ref[...] = kpe2.astype(BF16)
    cb = c.astype(BF16)

    nope_w = N_HEADS * QK_NOPE
    rope_w = N_HEADS * QK_ROPE
    if emit_q == "transposed":
        qn = _rms(q_lat, gq_ref[...]).astype(BF16)
        qall = _nt(wqup_ref[...], qn) * (SM_SCALE * LOG2E)
        half = QK_ROPE // 2
        c_t = cos_t_ref[...]
        s_t = sin_t_ref[...]
        zeros = jnp.zeros((QK_ROPE, rows), BF16)
        for h in range(N_HEADS):
            r1 = qall[nope_w + h * QK_ROPE:nope_w + h * QK_ROPE + half]
            r2 = qall[nope_w + h * QK_ROPE + half:nope_w + (h + 1) * QK_ROPE]
            r = jnp.concatenate([r1 * c_t - r2 * s_t, r1 * s_t + r2 * c_t], axis=0).astype(BF16)
            own = h * HEAD_W + QK_NOPE + (h % 2) * QK_ROPE
            other = h * HEAD_W + QK_NOPE + (1 - h % 2) * QK_ROPE
            q_ref[h * HEAD_W:h * HEAD_W + QK_NOPE, :] = qall[h * QK_NOPE:(h + 1) * QK_NOPE].astype(BF16)
            q_ref[own:own + QK_ROPE, :] = r
            q_ref[other:other + QK_ROPE, :] = zeros
    elif emit_q:
        qn = _rms(q_lat, gq_ref[...]).astype(BF16)
        qall = _mm(qn, wqup_ref[...]) * (SM_SCALE * LOG2E)
        for h in range(N_HEADS):
            j = h // 2
            r = qall[:, nope_w + j * LANES:nope_w + (j + 1) * LANES] * cos
            r = r + qall[:, nope_w + rope_w + j * LANES:nope_w + rope_w + (j + 1) * LANES] * sin
            q_ref[:, h * HEAD_W:h * HEAD_W + QK_NOPE] = qall[:, h * QK_NOPE:(h + 1) * QK_NOPE].astype(BF16)
            q_ref[:, h * HEAD_W + QK_NOPE:(h + 1) * HEAD_W] = jnp.where(half_masks[h % 2], r, 0.0).astype(BF16)

    if emit_kv:
        k_nope = _mm(cb, wuk_ref[...])
        for h in range(N_HEADS):
            k_ref[:, h * HEAD_W:h * HEAD_W + QK_NOPE] = k_nope[:, h * QK_NOPE:(h + 1) * QK_NOPE].astype(BF16)
            k_ref[:, h * HEAD_W + QK_NOPE:(h + 1) * HEAD_W] = jnp.where(half_masks[h % 2], kpe2, 0.0).astype(BF16)
        vt = _nt(wuvt_ref[...], cb)
        extra = V_ROWS - V_HEAD
        ones_rows = jnp.where(lax.broadcasted_iota(jnp.int32, (extra, rows), 0) == 0, 1.0, 0.0).astype(BF16)
        for h in range(N_HEADS):
            vt_ref[h * V_ROWS:h * V_ROWS + V_HEAD, :] = vt[h * V_HEAD:(h + 1) * V_HEAD].astype(BF16)
            vt_ref[h * V_ROWS + V_HEAD:(h + 1) * V_ROWS, :] = ones_rows

    if emit_gates:
        logits = _mm(xn, wg_ref[...])
        g_ref[...] = (0.5 * jnp.tanh(0.5 * logits) + 0.5).astype(BF16)

    u_ref[...] = _mm(xn, wu_ref[...])


def _project(x, rope, w, *, tile, emit_q, emit_kv, emit_gates, emit_kpe2, lead_rows=0):
    cos, sin, cos_t, sin_t = rope
    G, R, _ = x.shape
    nt = R // tile
    assert nt * tile == R
    row = lambda width: pl.BlockSpec((None, tile, width), lambda g, t: (g, t, 0))
    tab = pl.BlockSpec((tile, LANES), lambda g, t: (t, 0))
    q_t = emit_q == "transposed"
    weights = (w["g_mix"], w["g_q"], w["g_kv"], w["w_qlat"], w["w_kvr"], w["w_u"], w["w_g"],
               w["w_qup_t"] if q_t else w["w_qup"], w["w_uk"], w["w_uvt"])
    tables = [cos, sin]
    in_specs = [row(D_MODEL), tab, tab]
    if q_t:
        tables += [cos_t, sin_t]
        in_specs += [pl.BlockSpec((QK_ROPE // 2, tile), lambda g, t: (0, t))] * 2
    in_specs += [_const_spec(a.shape) for a in weights]
    out_shape, out_specs = [], []

    def add(shape, spec, dtype):
        out_shape.append(jax.ShapeDtypeStruct(shape, dtype))
        out_specs.append(spec)

    if q_t:
        add((G, N_HEADS * HEAD_W, R), pl.BlockSpec((None, N_HEADS * HEAD_W, tile), lambda g, t: (g, 0, t)), BF16)
    elif emit_q:
        add((G, R, N_HEADS * HEAD_W), row(N_HEADS * HEAD_W), BF16)
    if emit_kv:
        add((G, nt, tile, N_HEADS * HEAD_W),
            pl.BlockSpec((None, None, tile, N_HEADS * HEAD_W), lambda g, t: (g, t, 0, 0)), BF16)
        add((G, nt, N_HEADS * V_ROWS, tile),
            pl.BlockSpec((None, None, N_HEADS * V_ROWS, tile), lambda g, t: (g, t, 0, 0)), BF16)
    cache = row if lead_rows == 0 else (lambda width: pl.BlockSpec(
        (pl.Element(1), pl.Element(tile), pl.Element(width)),
        lambda g, t: (g, pl.multiple_of(lead_rows + t * tile, math.gcd(lead_rows, tile)), 0)))
    add((G, lead_rows + R, KV_LORA), cache(KV_LORA), F32)
    add((G, lead_rows + R, QK_ROPE), cache(QK_ROPE), F32)
    add((G, R, D_POOL), row(D_POOL), F32)
    if emit_gates:
        add((G, R, 2 * D_MODEL), row(2 * D_MODEL), BF16)
    if emit_kpe2:
        add((G, R, LANES), row(LANES), BF16)

    body = functools.partial(_project_body, emit_q=emit_q, emit_kv=emit_kv, emit_gates=emit_gates,
                             emit_kpe2=emit_kpe2)
    outs = pl.pallas_call(
        body,
        grid=(G, nt),
        in_specs=in_specs,
        out_specs=out_specs,
        out_shape=out_shape,
        compiler_params=pltpu.CompilerParams(dimension_semantics=("parallel", "parallel"),
                                             vmem_limit_bytes=VMEM_LIMIT),
        name="project",
    )(x, *tables, *weights)
    outs = list(outs)
    res = {}
    if emit_q:
        res["q"] = outs.pop(0)
    if emit_kv:
        res["k"], res["vt"] = outs.pop(0), outs.pop(0)
    res["c"], res["kpe"], res["u"] = outs.pop(0), outs.pop(0), outs.pop(0)
    if emit_gates:
        res["gates"] = outs.pop(0)
    if emit_kpe2:
        res["kpe2"] = outs.pop(0)
    return res


def _fill_lead_body(big_ref, rows_ref, out_ref):
    del big_ref
    out_ref[...] = rows_ref[...]


def _fill_lead_rows(big, rows):
    G, _, W = big.shape
    n = rows.shape[0]
    return pl.pallas_call(
        _fill_lead_body,
        grid=(G,),
        in_specs=[pl.BlockSpec(memory_space=pl.ANY), pl.BlockSpec((n, W), lambda g: (0, 0))],
        out_specs=pl.BlockSpec((None, n, W), lambda g: (g, 0, 0)),
        out_shape=jax.ShapeDtypeStruct(big.shape, big.dtype),
        input_output_aliases={0: 0},
        name="fill_lead_rows",
    )(big, rows)


def _flash_body(qt_ref, k_ref, vt_ref, km_ref, vtm_ref, o_ref, m_ref, acc_ref, s_ref, cm_ref, *, streams):
    tk = k_ref.shape[1]
    base = pl.program_id(2) * streams

    def produce(buf, kb, ss, slot=None):
        for s in ss:
            dst = s if slot is None else slot
            s_t = _mm(k_ref[kb], qt_ref[:, s * tk:(s + 1) * tk])
            s_ref[buf, dst] = s_t
            cm_ref[buf, dst] = jnp.max(s_t, axis=0, keepdims=True)

    def own_columns(buf, s):
        diag = (lax.broadcasted_iota(jnp.int32, (LANES, LANES), 0) // CHUNK
                <= lax.broadcasted_iota(jnp.int32, (LANES, LANES), 1) // CHUNK)
        cols = []
        for j in range(tk // LANES):
            tiles = [s_ref[buf, s, i * LANES:(i + 1) * LANES, j * LANES:(j + 1) * LANES] for i in range(j + 1)]
            tiles[j] = jnp.where(diag, tiles[j], NEG_INF)
            cols.append(jnp.concatenate(tiles, axis=0))
        return cols

    def consume(buf, s, kb, own=False, slot=None):
        slot = s if slot is None else slot
        m_old = m_ref[s]
        if own:
            cols = own_columns(buf, slot)
            m_blk = jnp.concatenate([jnp.max(c, axis=0, keepdims=True) for c in cols], axis=1)
            m_new = jnp.maximum(m_old, m_blk)
            p_cols = []
            for j, c in enumerate(cols):
                p_j = jnp.exp2(c - m_new[:, j * LANES:(j + 1) * LANES]).astype(BF16)
                if c.shape[0] < tk:
                    p_j = jnp.concatenate([p_j, jnp.zeros((tk - c.shape[0], LANES), BF16)], axis=0)
                p_cols.append(p_j)
            p = jnp.concatenate(p_cols, axis=1)
        else:
            m_new = jnp.maximum(m_old, cm_ref[buf, slot])
            p = jnp.exp2(s_ref[buf, slot] - m_new).astype(BF16)
        acc_ref[s] = jnp.exp2(m_old - m_new) * acc_ref[s] + _mm(vt_ref[kb], p)
        m_ref[s] = m_new

    everyone = range(streams)

    s_m = _mm(km_ref[...], qt_ref[...])
    produce(0, 0, everyone)
    s_m = jnp.where(lax.broadcasted_iota(jnp.int32, s_m.shape, 0) < N_META, s_m, NEG_INF)
    m_0 = jnp.max(s_m, axis=0, keepdims=True)
    acc_0 = _mm(vtm_ref[...], jnp.exp2(s_m - m_0).astype(BF16))
    for s in everyone:
        m_ref[s] = m_0[:, s * tk:(s + 1) * tk]
        acc_ref[s] = acc_0[:, s * tk:(s + 1) * tk]

    def step(rd, wr, kb):
        lead = 0
        for i in range(streams + lead):
            if i < streams:
                produce(wr, kb + 1, [i])
            if i >= lead:
                consume(rd, i - lead, kb)

    per_trip = 4 if streams % 4 == 0 else 2

    def trip(j, carry):
        for i in range(per_trip):
            step(i % 2, 1 - i % 2, per_trip * j + i)
        return carry

    lax.fori_loop(0, base // per_trip, trip, 0)

    def finish(s):
        acc = acc_ref[s]
        o_ref[s * tk:(s + 1) * tk, :] = (acc[:V_HEAD] / acc[V_HEAD:V_HEAD + 1]).T.astype(BF16)

    for s in range(1, streams):
        produce(1, base + s, [s])
        consume(0, s - 1, base, own=(s == 1))
    consume(0, streams - 1, base)
    finish(0)
    left = [(s, base + j) for s in range(2, streams) for j in range(1, s)]
    assert len(left) <= streams
    for slot, (s, kb) in enumerate(left):
        produce(0, kb, [s], slot=slot)
        if slot + 1 < streams:
            consume(1, slot + 1, base + slot + 1, own=True)
    for s in range(len(left) + 1, streams):
        consume(1, s, base + s, own=True)
    finish(1)
    for slot, (s, kb) in enumerate(left):
        consume(0, s, kb, slot=slot)
        if slot + 1 == len(left) or left[slot + 1][0] != s:
            finish(s)


def _flash(q, k, vt, k_meta, vt_meta):
    B, _, S = q.shape
    nt, tile = k.shape[1], k.shape[2]
    streams = Q_STREAMS
    assert nt % streams == 0 and streams % 2 == 0
    return pl.pallas_call(
        functools.partial(_flash_body, streams=streams),
        grid=(B, N_HEADS, nt // streams),
        in_specs=[
            pl.BlockSpec((None, HEAD_W, streams * tile), lambda b, h, i: (b, h, i)),
            pl.BlockSpec((None, nt, tile, HEAD_W), lambda b, h, i: (b, 0, 0, h)),
            pl.BlockSpec((None, nt, V_ROWS, tile), lambda b, h, i: (b, 0, h, 0)),
            pl.BlockSpec((None, None, META_PAD, HEAD_W), lambda b, h, i: (0, 0, 0, h)),
            pl.BlockSpec((None, None, V_ROWS, META_PAD), lambda b, h, i: (0, 0, h, 0)),
        ],
        out_specs=pl.BlockSpec((None, streams * tile, V_HEAD), lambda b, h, i: (b, i, h)),
        out_shape=jax.ShapeDtypeStruct((B, S, ATTN_WIDTH), BF16),
        scratch_shapes=[pltpu.VMEM((streams, 1, tile), F32),
                        pltpu.VMEM((streams, V_ROWS, tile), F32),
                        pltpu.VMEM((2, streams, tile, tile), F32), pltpu.VMEM((2, streams, 1, tile), F32)],
        compiler_params=pltpu.CompilerParams(dimension_semantics=("parallel", "parallel", "arbitrary"),
                                             vmem_limit_bytes=VMEM_LIMIT),
        name="flash",
    )(q, k, vt, k_meta, vt_meta)


def _decode_body(q_ref, cn_ref, pn_ref, cc_ref, pc_ref, wukt_ref, wuv_ref, o_ref):
    n = cc_ref.shape[0]
    t = q_ref.shape[0] // n
    seqs = range(n)
    rows = [slice(i * t, (i + 1) * t) for i in seqs]
    q = [q_ref[r, :] for r in rows]
    qa = [jnp.concatenate([_mm(q[i][:, h * HEAD_W:h * HEAD_W + QK_NOPE], wukt_ref[h]) for h in range(N_HEADS)],
                          axis=0).astype(BF16) for i in seqs]
    qp = [jnp.concatenate([q[i][:, h * HEAD_W + QK_NOPE:(h + 1) * HEAD_W] for h in range(N_HEADS)], axis=0)
          for i in seqs]
    cc = [cc_ref[i].astype(BF16) for i in seqs]
    cn = [cn_ref[r, :].astype(BF16) for r in rows]
    pc2 = [jnp.concatenate([pc_ref[i].astype(BF16)] * (LANES // QK_ROPE), axis=0) for i in seqs]
    s_c = [_nt(qa[i], cc[i]) + _mm(qp[i], pc2[i]) for i in seqs]
    s_n = [_nt(qa[i], cn[i]) + _nt(qp[i], pn_ref[rows[i], :]) for i in seqs]
    o_lat = []
    for i in seqs:
        m = jnp.maximum(jnp.max(s_c[i], axis=-1, keepdims=True), jnp.max(s_n[i], axis=-1, keepdims=True))
        p_c = jnp.exp2(s_c[i] - m)
        p_n = jnp.exp2(s_n[i] - m)
        l = jnp.sum(p_c, axis=-1, keepdims=True) + jnp.sum(p_n, axis=-1, keepdims=True)
        o_lat.append(((_mm(p_c.astype(BF16), cc[i]) + _mm(p_n.astype(BF16), cn[i])) / l).astype(BF16))
    for i in seqs:
        for h in range(N_HEADS):
            o_ref[rows[i], h * V_HEAD:(h + 1) * V_HEAD] = _mm(o_lat[i][h * t:(h + 1) * t], wuv_ref[h]).astype(BF16)


def _decode(q, c_new, kpe2_new, cache_c, cache_pe_t, wukt, wuv, *, seq):
    R = q.shape[0]
    nb, past, _ = cache_c.shape
    per = DECODE_SEQS
    assert nb * seq == R and nb % per == 0
    return pl.pallas_call(
        _decode_body,
        grid=(nb // per,),
        in_specs=[
            pl.BlockSpec((per * seq, N_HEADS * HEAD_W), lambda i: (i, 0)),
            pl.BlockSpec((per * seq, KV_LORA), lambda i: (i, 0)),
            pl.BlockSpec((per * seq, LANES), lambda i: (i, 0)),
            pl.BlockSpec((per, past, KV_LORA), lambda i: (i, 0, 0)),
            pl.BlockSpec((per, QK_ROPE, past), lambda i: (i, 0, 0)),
            _const_spec(wukt.shape),
            _const_spec(wuv.shape),
        ],
        out_specs=pl.BlockSpec((per * seq, ATTN_WIDTH), lambda i: (i, 0)),
        out_shape=jax.ShapeDtypeStruct((R, ATTN_WIDTH), BF16),
        compiler_params=pltpu.CompilerParams(dimension_semantics=("parallel",),
                                             vmem_limit_bytes=VMEM_LIMIT),
        name="decode",
    )(q, c_new, kpe2_new, cache_c, cache_pe_t, wukt, wuv)


def _pool_seq(hist, u):
    ext = jnp.concatenate([hist, u], axis=0)
    outs = []
    for g, w in enumerate(POOL_WINDOWS):
        e = ext[:, g * POOL_GROUP:(g + 1) * POOL_GROUP]
        s = e
        shift = 1
        while shift < w:
            s = s + pltpu.roll(s, shift, axis=0)
            shift *= 2
        outs.append(s[HIST_ROWS:] * (1.0 / w) - e[HIST_ROWS:])
    return outs


def _merge_body(*refs, n_seq, prompt):
    if prompt:
        (x_ref, o_ref, u_ref, hprev_ref, hmeta_ref, g_ref, wab_ref, wpg_ref, ps_ref, wpb_ref, wout_ref,
         gffn_ref, wup_ref, wdown_ref, gfin_ref, y_ref) = refs
        first = pl.program_id(1) == 0
        hists = [jnp.where(first, hmeta_ref[...], hprev_ref[...])]
    else:
        (x_ref, o_ref, u_ref, hist_ref, g_ref, wab_ref, wpg_ref, ps_ref, wpb_ref, wout_ref,
         gffn_ref, wup_ref, wdown_ref, gfin_ref, y_ref) = refs
        hists = [hist_ref[i] for i in range(n_seq)]

    rows = x_ref.shape[0]
    t = rows // n_seq
    halves = [slice(0, rows // 2), slice(rows // 2, rows)]
    a = [_mm(o_ref[hs, :], wab_ref[...]) for hs in halves]
    u = u_ref[...]
    per_seq = [_pool_seq(hists[i], u[i * t:(i + 1) * t]) for i in range(n_seq)]
    mixed = []
    for g in range(len(POOL_WINDOWS)):
        pooled = jnp.concatenate([per_seq[i][g] for i in range(n_seq)], axis=0) if n_seq > 1 else per_seq[0][g]
        mixed.append(_mm(pooled.astype(BF16), wpg_ref[g]))
    pm = (jnp.concatenate(mixed, axis=-1) * ps_ref[...]).astype(BF16)
    p = [_mm(pm[hs], wpb_ref[...]) for hs in halves]
    h = []
    for i, hs in enumerate(halves):
        gates = g_ref[hs, :].astype(F32)
        mix = (gates[:, :D_MODEL] * a[i] + gates[:, D_MODEL:] * p[i]).astype(BF16)
        h.append(x_ref[hs, :] + _mm(mix, wout_ref[...]))
    hn = [_rms(h_i, gffn_ref[...]).astype(BF16) for h_i in h]
    for c in range(D_FF // FF_CHUNK):
        cols = slice(c * FF_CHUNK, (c + 1) * FF_CHUNK)
        up = [jnp.maximum(_mm(hn_i, wup_ref[:, cols]), 0.0) for hn_i in hn]
        h = [h_i + _mm((up_i * up_i).astype(BF16), wdown_ref[cols, :]) for h_i, up_i in zip(h, up)]
    for hs, h_i in zip(halves, h):
        y_ref[hs, :] = _rms(h_i, gfin_ref[...])


def _merge_ffn(x, o, u, gates, hist_args, w, *, tile, n_seq, prompt):
    G, R, _ = x.shape
    nt = R // tile
    row = lambda width: pl.BlockSpec((None, tile, width), lambda g, t: (g, t, 0))
    if prompt:
        (u_meta,) = hist_args
        per_tile = tile // HIST_ROWS
        hist_specs = [
            pl.BlockSpec((None, HIST_ROWS, D_POOL), lambda g, t: (g, jnp.maximum(t * per_tile - 1, 0), 0)),
            pl.BlockSpec((HIST_ROWS, D_POOL), lambda g, t: (0, 0)),
        ]
        hist_in = [u, u_meta]
    else:
        (hist,) = hist_args
        hist_specs = [pl.BlockSpec((n_seq, HIST_ROWS, D_POOL), lambda g, t: (t, 0, 0))]
        hist_in = [hist]
    weights = (w["w_attn_br"], w["w_pool_grp"], w["pool_scale"], w["w_pool_br"], w["w_out"],
               w["g_ffn"], w["w_up"], w["w_down"], w["g_final"])
    in_specs = ([row(D_MODEL), row(ATTN_WIDTH), row(D_POOL)] + hist_specs + [row(2 * D_MODEL)]
                + [_const_spec(a.shape) for a in weights])
    return pl.pallas_call(
        functools.partial(_merge_body, n_seq=n_seq, prompt=prompt),
        grid=(G, nt),
        in_specs=in_specs,
        out_specs=row(D_MODEL),
        out_shape=jax.ShapeDtypeStruct((G, R, D_MODEL), F32),
        compiler_params=pltpu.CompilerParams(dimension_semantics=("parallel", "parallel"),
                                             vmem_limit_bytes=VMEM_LIMIT),
        name="merge_ffn",
    )(x, o, u, *hist_in, gates, *weights)


def _rope_tables(base_pos, off_pos):
    half = QK_ROPE // 2
    inv_freq = jnp.exp(-math.log(ROPE_BASE) * jnp.arange(half, dtype=jnp.float32) / half)
    reps = LANES // half
    freq = jnp.tile(inv_freq, reps)
    sign = jnp.tile(jnp.concatenate([-jnp.ones(half, F32), jnp.ones(half, F32)]), reps // 2)
    a = base_pos.astype(jnp.float32)[:, None, None] * freq
    b = off_pos.astype(jnp.float32)[None, :, None] * freq
    cos = (jnp.cos(a) * jnp.cos(b) - jnp.sin(a) * jnp.sin(b)).reshape(-1, LANES)
    sin = ((jnp.sin(a) * jnp.cos(b) + jnp.cos(a) * jnp.sin(b)) * sign).reshape(-1, LANES)
    a_t = inv_freq[:, None, None] * base_pos.astype(jnp.float32)[None, :, None]
    b_t = inv_freq[:, None, None] * off_pos.astype(jnp.float32)[None, None, :]
    cos_t = (jnp.cos(a_t) * jnp.cos(b_t) - jnp.sin(a_t) * jnp.sin(b_t)).reshape(half, -1)
    sin_t = (jnp.sin(a_t) * jnp.cos(b_t) + jnp.cos(a_t) * jnp.sin(b_t)).reshape(half, -1)
    return cos, sin, cos_t, sin_t


def _prep_weights(l, w_in, g_norm_mix, g_q, g_kv, w_q_up, w_uk, w_uv, w_attn_br, w_pool_grp,
                  pool_scale, w_pool_br, w_out, g_norm_ffn, w_up, w_down, g_final):
    half = QK_ROPE // 2
    i0, i1, i2, i3 = Q_LORA, Q_LORA + KV_LORA, Q_LORA + KV_LORA + QK_ROPE, Q_LORA + KV_LORA + QK_ROPE + D_POOL
    wi = w_in[l]
    w_kr = wi[:, i1:i2]
    w_kr_sw = jnp.concatenate([w_kr[:, half:], w_kr[:, :half]], axis=-1)
    wq = w_q_up[l].reshape(Q_LORA, N_HEADS, QK_NOPE + QK_ROPE)
    wq_r1, wq_r2 = wq[:, :, QK_NOPE:QK_NOPE + half], wq[:, :, QK_NOPE + half:]
    row = lambda v: v.reshape(1, -1).astype(F32)
    w_qup = jnp.concatenate([
        wq[:, :, :QK_NOPE].reshape(Q_LORA, -1),
        jnp.concatenate([wq_r1, wq_r2], axis=-1).reshape(Q_LORA, -1),
        jnp.concatenate([wq_r2, wq_r1], axis=-1).reshape(Q_LORA, -1)], axis=-1).astype(BF16)
    return {
        "g_mix": row(g_norm_mix[l]), "g_q": row(g_q[l]), "g_kv": row(g_kv[l]),
        "w_qlat": wi[:, :i0].astype(BF16),
        "w_kvr": jnp.concatenate([wi[:, i0:i1], w_kr, w_kr, w_kr_sw, w_kr_sw], axis=-1).astype(BF16),
        "w_u": wi[:, i2:i3].astype(BF16),
        "w_g": wi[:, i3:].astype(BF16),
        "w_qup": w_qup,
        "w_qup_t": w_qup[:, :N_HEADS * (QK_NOPE + QK_ROPE)].T,
        "w_uk": w_uk[l].reshape(KV_LORA, N_HEADS * QK_NOPE).astype(BF16),
        "w_uvt": w_uv[l].reshape(KV_LORA, N_HEADS * V_HEAD).T.astype(BF16),
        "w_ukt": jnp.transpose(w_uk[l], (1, 2, 0)).astype(BF16),
        "w_uv3": jnp.transpose(w_uv[l], (1, 0, 2)).astype(BF16),
        "w_attn_br": w_attn_br[l].astype(BF16),
        "w_pool_grp": w_pool_grp[l].astype(BF16),
        "pool_scale": row(pool_scale[l]),
        "w_pool_br": w_pool_br[l].astype(BF16),
        "w_out": w_out[l].astype(BF16),
        "g_ffn": row(g_norm_ffn[l]),
        "w_up": w_up[l].astype(BF16),
        "w_down": w_down[l].astype(BF16),
        "g_final": row(g_final),
    }


def kernel(x_prompt, x_sample, cache_kv_latent, cache_k_rope, cache_pool, meta_tokens, w_in, g_norm_mix, g_q, g_kv, w_q_up, w_uk, w_uv, w_attn_br, w_pool_grp, pool_scale, w_pool_br, w_out, g_norm_ffn, w_up, w_down, g_final):
    B, S, _ = x_prompt.shape
    Bd, T, _ = x_sample.shape
    past = cache_kv_latent.shape[2]
    assert w_in.shape[0] == 1 and S % ROW_TILE == 0 and (Bd * T) % ROW_TILE == 0 and ROW_TILE % T == 0
    assert N_META >= max(POOL_WINDOWS) and POOL_HIST + 1 >= max(POOL_WINDOWS) and T >= POOL_HIST
    w = _prep_weights(0, w_in, g_norm_mix, g_q, g_kv, w_q_up, w_uk, w_uv, w_attn_br, w_pool_grp,
                      pool_scale, w_pool_br, w_out, g_norm_ffn, w_up, w_down, g_final)

    meta = jnp.concatenate([meta_tokens.astype(F32), jnp.zeros((META_PAD - N_META, D_MODEL), F32)], axis=0)
    iota = functools.partial(jnp.arange, dtype=jnp.int32)
    rope_m = _rope_tables(jnp.zeros((1,), jnp.int32), iota(META_PAD))
    pm = _project(meta[None], rope_m, w, tile=META_PAD, emit_q=False, emit_kv=True,
                  emit_gates=False, emit_kpe2=False)

    rope_p = _rope_tables(N_META + ROW_TILE * iota(S // ROW_TILE), iota(ROW_TILE))
    pp = _project(x_prompt, rope_p, w, tile=ROW_TILE, emit_q="transposed", emit_kv=True,
                  emit_gates=True, emit_kpe2=False, lead_rows=N_META)
    o_p = _flash(pp["q"], pp["k"], pp["vt"], pm["k"], pm["vt"])
    y_prompt = _merge_ffn(x_prompt, o_p, pp["u"], pp["gates"], (pm["u"][0],), w,
                          tile=ROW_TILE, n_seq=1, prompt=True)

    rs = Bd * T
    rope_s = _rope_tables(jnp.full((Bd,), past, jnp.int32), iota(T))
    xs = x_sample.reshape(1, rs, D_MODEL)
    ps = _project(xs, rope_s, w, tile=ROW_TILE, emit_q=True, emit_kv=False,
                  emit_gates=True, emit_kpe2=True)
    cache_pe_t = jnp.swapaxes(cache_k_rope[0], 1, 2)
    o_s = _decode(ps["q"][0], ps["c"][0], ps["kpe2"][0], cache_kv_latent[0], cache_pe_t,
                  w["w_ukt"], w["w_uv3"], seq=T)
    hist_s = jnp.concatenate([jnp.zeros((Bd, HIST_ROWS - POOL_HIST, D_POOL), F32), cache_pool[0]], axis=1)
    y_sample = _merge_ffn(xs, o_s[None], ps["u"], ps["gates"], (hist_s,), w,
                          tile=ROW_TILE, n_seq=ROW_TILE // T, prompt=False).reshape(Bd, T, D_MODEL)

    def with_meta(m, f):
        return _fill_lead_rows(f, m[0, :N_META])[None]

    c_s = ps["c"].reshape(Bd, T, KV_LORA)
    pe_s = ps["kpe"].reshape(Bd, T, QK_ROPE)
    u_s = ps["u"].reshape(Bd, T, D_POOL)
    return (y_prompt, y_sample,
            with_meta(pm["c"], pp["c"]), with_meta(pm["kpe"], pp["kpe"]), pp["u"][:, -POOL_HIST:][None],
            c_s[None], pe_s[None], u_s[:, -POOL_HIST:][None])
```
